```python
import math
import jax, jax.numpy as jnp
from jax import lax
import numpy as np

D_MODEL = 1024
BATCH = 8
SEQ = 2048
DEPTH = 1
DEC_BATCH = 128
DEC_SEQ = 4
PAST_LEN = 16384
PAGE_SIZE = 128

MIX_WIDTH = D_MODEL
HGRN_WIDTH = MIX_WIDTH // 2
RET_WIDTH = MIX_WIDTH - HGRN_WIDTH
HGRN_HEADS = 4
HGRN_HEAD_DIM = HGRN_WIDTH // HGRN_HEADS
RET_HEADS = 4
RET_HEAD_DIM = RET_WIDTH // RET_HEADS
IN_SIZES = [HGRN_WIDTH] * 4 + [RET_WIDTH] * 4
IN_COLS = sum(IN_SIZES)
D_FF = ((8 * D_MODEL // 3 + 255) // 256) * 256
CHUNK = 64
ROPE_BASE = 10000.0
DEEPNORM_ALPHA = (2.0 * DEPTH) ** 0.25
DEEPNORM_BETA = (8.0 * DEPTH) ** -0.25
EPS = 1e-5

kernel_name = "hymba_hgrn2_retnet_deepnorm_step"

F32 = jnp.float32


def _layer_norm(x, g, b):
    xf = x.astype(F32)
    mu = jnp.mean(xf, -1, keepdims=True)
    var = jnp.mean(jnp.square(xf - mu), -1, keepdims=True)
    return ((xf - mu) * lax.rsqrt(var + EPS) * g.astype(F32) + b.astype(F32)).astype(x.dtype)


def _head_rmsnorm(o, g):
    return o * lax.rsqrt(jnp.mean(jnp.square(o), -1, keepdims=True) + EPS) * g.astype(F32)


def _rotary(x, pos):
    half = x.shape[-1] // 2
    inv = ROPE_BASE ** (-jnp.arange(half, dtype=F32) / half)
    ang = pos[:, None] * inv[None, :]
    cos = jnp.cos(ang)[None, :, None, :]
    sin = jnp.sin(ang)[None, :, None, :]
    x1, x2 = x[..., :half], x[..., half:]
    return jnp.concatenate([x1 * cos - x2 * sin, x1 * sin + x2 * cos], axis=-1)


def _to_chunks(x, c):
    B, T, H, D = x.shape
    return x.reshape(B, T // c, c, H, D).transpose(1, 0, 3, 2, 4)


def _from_chunks(x):
    n, B, H, c, D = x.shape
    return x.transpose(1, 0, 3, 2, 4).reshape(B, n * c, H, D)


def _hgrn2_chunkwise(q, k, v, logf, s0):
    T = q.shape[1]
    c = math.gcd(T, CHUNK)
    m5 = jnp.tril(jnp.ones((c, c), dtype=bool))[:, :, None]

    def step(S, inp):
        qc, kc, vc, gc = inp
        b = jnp.cumsum(gc, axis=2)
        diff = b[:, :, :, None, :] - b[:, :, None, :, :]
        decay = jnp.exp(jnp.where(m5, diff, -jnp.inf))
        A = jnp.einsum('bhid,bhjd,bhijd->bhij', qc, kc, decay)
        o = (jnp.einsum('bhij,bhjv->bhiv', A, vc)
             + jnp.einsum('bhid,bhdv->bhiv', qc * jnp.exp(b), S))
        b_last = b[:, :, -1:, :]
        S_new = (jnp.exp(b_last[:, :, 0, :])[..., None] * S
                 + jnp.einsum('bhjd,bhjv->bhdv', kc * jnp.exp(b_last - b), vc))
        return S_new, o

    S, o = lax.scan(step, s0, (_to_chunks(q, c), _to_chunks(k, c), _to_chunks(v, c), _to_chunks(logf, c)))
    return _from_chunks(o), S


def _retention_chunkwise(q, k, v, s0):
    T, H = q.shape[1], q.shape[2]
    c = math.gcd(T, CHUNK)
    lg = jnp.log1p(-jnp.exp2(-5.0 - jnp.arange(H, dtype=F32)))
    idx = jnp.arange(c, dtype=F32)
    mask = jnp.tril(jnp.ones((c, c), dtype=bool))
    rel = jnp.where(mask, idx[:, None] - idx[None, :], 0.0)
    dmat = jnp.where(mask[None], jnp.exp(lg[:, None, None] * rel[None]), 0.0)
    cross = jnp.exp(lg[:, None] * (idx + 1.0)[None, :])
    upd = jnp.exp(lg[:, None] * (c - 1.0 - idx)[None, :])
    chunk_dec = jnp.exp(lg * c)

    def step(S, inp):
        qc, kc, vc = inp
        A = jnp.einsum('bhid,bhjd->bhij', qc, kc) * dmat[None]
        o = (jnp.einsum('bhij,bhjv->bhiv', A, vc)
             + jnp.einsum('bhid,bhdv->bhiv', qc, S) * cross[None, :, :, None])
        S_new = (chunk_dec[None, :, None, None] * S
                 + jnp.einsum('bhjd,bhjv->bhdv', kc * upd[None, :, :, None], vc))
        return S_new, o

    S, o = lax.scan(step, s0, (_to_chunks(q, c), _to_chunks(k, c), _to_chunks(v, c)))
    return _from_chunks(o), S


def _mixer(x, pos, s_hgrn, s_ret, w_in, lb, g_hgrn, g_ret, w_out):
    B, T, _ = x.shape
    proj = jnp.einsum('btd,dc->btc', x, w_in).astype(F32)
    hq, hf, hi, hg, rq, rk, rv, rg = jnp.split(proj, list(np.cumsum(IN_SIZES)[:-1]), axis=-1)

    def heads(t, h):
        return t.reshape(B, T, h, -1)

    f = lb + (1.0 - lb) * jax.nn.sigmoid(hf)
    logf = jnp.log(f)
    k_h = (1.0 - lb) * jax.nn.sigmoid(-hf)
    o_h, s_h = _hgrn2_chunkwise(heads(jax.nn.silu(hq), HGRN_HEADS), heads(k_h, HGRN_HEADS),
                                heads(hi, HGRN_HEADS), heads(logf, HGRN_HEADS), s_hgrn.astype(F32))
    o_h = _head_rmsnorm(o_h, g_hgrn) * heads(jax.nn.silu(hg), HGRN_HEADS)

    q_r = _rotary(heads(rq, RET_HEADS), pos)
    k_r = _rotary(heads(rk, RET_HEADS), pos) * (RET_HEAD_DIM ** -0.5)
    o_r, s_r = _retention_chunkwise(q_r, k_r, heads(rv, RET_HEADS), s_ret.astype(F32))
    o_r = _head_rmsnorm(o_r, g_ret) * heads(jax.nn.silu(rg), RET_HEADS)

    mix = jnp.concatenate([o_h.reshape(B, T, HGRN_WIDTH), o_r.reshape(B, T, RET_WIDTH)], axis=-1)
    out = jnp.einsum('btc,cd->btd', mix.astype(x.dtype), w_out)
    return out, s_h, s_r


def _trunk(x, pos, state_hgrn, state_ret, w_in, lb_logits, hgrn_norm_g, ret_norm_g, w_out,
           ln1_g, ln1_b, w_gate, w_up, w_down, ln2_g, ln2_b):
    lb_all = jnp.cumsum(jax.nn.softmax(lb_logits.astype(F32), axis=0), axis=0)
    new_h, new_r = [], []
    for l in range(DEPTH):
        m, sh, sr = _mixer(x, pos, state_hgrn[l], state_ret[l], w_in[l], lb_all[l],
                           hgrn_norm_g[l], ret_norm_g[l], w_out[l])
        x = _layer_norm(DEEPNORM_ALPHA * x + m, ln1_g[l], ln1_b[l])
        hidden = jax.nn.silu(jnp.einsum('btd,df->btf', x, w_gate[l])) * jnp.einsum('btd,df->btf', x, w_up[l])
        x = _layer_norm(DEEPNORM_ALPHA * x + jnp.einsum('btf,fd->btd', hidden, w_down[l]), ln2_g[l], ln2_b[l])
        new_h.append(sh.astype(state_hgrn.dtype))
        new_r.append(sr.astype(state_ret.dtype))
    return x, jnp.stack(new_h), jnp.stack(new_r)


def setup_inputs(seed: int = 0) -> dict:
    key = jax.random.key(seed)
    ks = jax.random.split(key, 20)
    col_scale = np.ones((IN_COLS,), np.float32)
    offs = np.cumsum([0] + IN_SIZES)
    col_scale[offs[2]:offs[3]] = DEEPNORM_BETA
    col_scale[offs[6]:offs[7]] = DEEPNORM_BETA
    w_in = jax.random.normal(ks[0], (DEPTH, D_MODEL, IN_COLS), F32) * (D_MODEL ** -0.5) * jnp.asarray(col_scale)
    return {
        "x_prompt": jax.random.normal(ks[1], (BATCH, SEQ, D_MODEL), F32),
        "x_sample": jax.random.normal(ks[2], (DEC_BATCH, DEC_SEQ, D_MODEL), F32),
        "state_hgrn": 0.5 * jax.random.normal(ks[3], (DEPTH, DEC_BATCH, HGRN_HEADS, HGRN_HEAD_DIM, HGRN_HEAD_DIM), F32),
        "state_ret": 0.5 * jax.random.normal(ks[4], (DEPTH, DEC_BATCH, RET_HEADS, RET_HEAD_DIM, RET_HEAD_DIM), F32),
        "w_in": w_in,
        "lb_logits": 0.5 * jax.random.normal(ks[5], (DEPTH + 1, HGRN_WIDTH), F32),
        "hgrn_norm_g": 1.0 + 0.02 * jax.random.normal(ks[6], (DEPTH, HGRN_HEADS, HGRN_HEAD_DIM), F32),
        "ret_norm_g": 1.0 + 0.02 * jax.random.normal(ks[7], (DEPTH, RET_HEADS, RET_HEAD_DIM), F32),
        "w_out": jax.random.normal(ks[8], (DEPTH, MIX_WIDTH, D_MODEL), F32) * (MIX_WIDTH ** -0.5) * DEEPNORM_BETA,
        "ln1_g": 1.0 + 0.02 * jax.random.normal(ks[9], (DEPTH, D_MODEL), F32),
        "ln1_b": 0.02 * jax.random.normal(ks[10], (DEPTH, D_MODEL), F32),
        "w_gate": jax.random.normal(ks[11], (DEPTH, D_MODEL, D_FF), F32) * (D_MODEL ** -0.5) * DEEPNORM_BETA,
        "w_up": jax.random.normal(ks[12], (DEPTH, D_MODEL, D_FF), F32) * (D_MODEL ** -0.5) * DEEPNORM_BETA,
        "w_down": jax.random.normal(ks[13], (DEPTH, D_FF, D_MODEL), F32) * (D_FF ** -0.5) * DEEPNORM_BETA,
        "ln2_g": 1.0 + 0.02 * jax.random.normal(ks[14], (DEPTH, D_MODEL), F32),
        "ln2_b": 0.02 * jax.random.normal(ks[15], (DEPTH, D_MODEL), F32),
    }


def reference(x_prompt, x_sample, state_hgrn, state_ret, w_in, lb_logits, hgrn_norm_g, ret_norm_g,
              w_out, ln1_g, ln1_b, w_gate, w_up, w_down, ln2_g, ln2_b):
    B, T = x_prompt.shape[0], x_prompt.shape[1]
    Bd, Td = x_sample.shape[0], x_sample.shape[1]
    pos_prompt = jnp.arange(T, dtype=F32)
    pos_sample = jnp.arange(Td, dtype=F32) + float(PAST_LEN)
    zero_h = jnp.zeros((DEPTH, B, HGRN_HEADS, HGRN_HEAD_DIM, HGRN_HEAD_DIM), x_prompt.dtype)
    zero_r = jnp.zeros((DEPTH, B, RET_HEADS, RET_HEAD_DIM, RET_HEAD_DIM), x_prompt.dtype)
    y_prompt, hgrn_state_prompt, ret_state_prompt = _trunk(
        x_prompt, pos_prompt, zero_h, zero_r, w_in, lb_logits, hgrn_norm_g, ret_norm_g, w_out,
        ln1_g, ln1_b, w_gate, w_up, w_down, ln2_g, ln2_b)
    y_sample, hgrn_state_sample, ret_state_sample = _trunk(
        x_sample, pos_sample, state_hgrn, state_ret, w_in, lb_logits, hgrn_norm_g, ret_norm_g, w_out,
        ln1_g, ln1_b, w_gate, w_up, w_down, ln2_g, ln2_b)
    return (y_prompt, y_sample, hgrn_state_prompt, ret_state_prompt, hgrn_state_sample, ret_state_sample)
```

```python
import functools
import math

import numpy as np
import jax
import jax.numpy as jnp
from jax import lax
from jax.experimental import pallas as pl
from jax.experimental.pallas import tpu as pltpu

F32 = jnp.float32
BF16 = jnp.bfloat16

D_MODEL = 1024
HEADS = 4
HEAD_DIM = 128
WIDTH = HEADS * HEAD_DIM
IN_COLS = 8 * WIDTH
D_FF = 2816
CHUNK = 64
ROPE_BASE = 10000.0
PAST_LEN = 16384
DEEPNORM_ALPHA = 2.0 ** 0.25
EPS = 1e-5

LANES = 128
SUBLANES = 8
PROMPT_TILE = 256
TAIL_TILE = 256
SAMPLE_SEQS = 8
ROPE_TILE = 128
VMEM_LIMIT = 56 * 1024 * 1024

LEVELS = (32, 16, 8, 4, 2, 1)


def _sigmoid_pair(x):
    e = jnp.exp(-jnp.abs(x))
    r = 1.0 / (1.0 + e)
    er = e * r
    pos = x >= 0
    return jnp.where(pos, r, er), jnp.where(pos, er, r)


def _silu(x):
    s, _ = _sigmoid_pair(x)
    return x * s


def _dot(a, b):
    return jnp.dot(a, b, preferred_element_type=F32)


def _dot_nt(a, b):
    return lax.dot_general(a, b, (((1,), (1,)), ((), ())), preferred_element_type=F32)


def _dot_tn(a, b):
    return lax.dot_general(a, b, (((0,), (0,)), ((), ())), preferred_element_type=F32)


def _split3(x):
    hi = x.astype(BF16)
    r = x - hi.astype(F32)
    mid = r.astype(BF16)
    lo = (r - mid.astype(F32)).astype(BF16)
    return hi, mid, lo


def _lower_bound(lbl_ref):
    l = lbl_ref[...]
    l0 = l[0:1, :]
    l1 = l[1:2, :]
    m = jnp.maximum(l0, l1)
    e0 = jnp.exp(l0 - m)
    e1 = jnp.exp(l1 - m)
    return e0 / (e0 + e1)


def _head_norm_gate(o, gain, gate):
    ms = jnp.mean(o * o, axis=-1, keepdims=True)
    return o * lax.rsqrt(ms + EPS) * gain * gate


def _layer_norm(h, g, b):
    mu = jnp.mean(h, axis=-1, keepdims=True)
    d = h - mu
    var = jnp.mean(d * d, axis=-1, keepdims=True)
    return d * lax.rsqrt(var + EPS) * g + b


def _project_activate(xb, w_ref, lb, cos2, sin2, qh, kh, vh, gh, gth, qr, kr, vr, gtr):
    def proj(g):
        return _dot(xb, w_ref[:, g * WIDTH:(g + 1) * WIDTH])

    qh[...] = _silu(proj(0))
    s_pos, s_neg = _sigmoid_pair(proj(1))
    gh[...] = jnp.log(lb + (1.0 - lb) * s_pos)
    kh[...] = (1.0 - lb) * s_neg
    vh[...] = proj(2)
    gth[...] = _silu(proj(3))

    def rotary(p, scale):
        for h in range(HEADS):
            hs = slice(h * HEAD_DIM, (h + 1) * HEAD_DIM)
            xh = p[:, hs]
            r = xh * cos2 + pltpu.roll(xh, HEAD_DIM // 2, 1) * sin2
            yield hs, (r if scale is None else r * scale)

    for hs, r in rotary(proj(4), None):
        qr[:, hs] = r
    for hs, r in rotary(proj(5), HEAD_DIM ** -0.5):
        kr[:, hs] = r
    vr[...] = proj(6)
    gtr[...] = _silu(proj(7))


def _rope_kernel(pos_ref, inv_ref, sign_ref, cos_ref, sin_ref):
    ang = pos_ref[...] * inv_ref[...]
    cos_ref[...] = jnp.cos(ang)
    sin_ref[...] = jnp.sin(ang) * sign_ref[...]


def _rope_tables(pos):
    n = pos.shape[0]
    half = HEAD_DIM // 2
    inv = ROPE_BASE ** (-jnp.arange(half, dtype=F32) / half)
    inv2 = jnp.concatenate([inv, inv])[None, :]
    sign = jnp.asarray(np.concatenate([-np.ones(half), np.ones(half)])[None, :], F32)
    tbl = pl.BlockSpec((ROPE_TILE, HEAD_DIM), lambda i: (i, 0))
    row = pl.BlockSpec((1, HEAD_DIM), lambda i: (0, 0))
    return pl.pallas_call(
        _rope_kernel,
        grid=(n // ROPE_TILE,),
        in_specs=[pl.BlockSpec((ROPE_TILE, 1), lambda i: (i, 0)), row, row],
        out_specs=[tbl, tbl],
        out_shape=(jax.ShapeDtypeStruct((n, HEAD_DIM), F32),) * 2,
        name="rope_tables",
    )(pos[:, None], inv2, sign)


def _ret_consts(c):
    lg = jnp.log1p(-jnp.exp2(-5.0 - jnp.arange(HEADS, dtype=F32)))
    idx = jnp.arange(c, dtype=F32)
    mask = jnp.tril(jnp.ones((c, c), dtype=bool))
    rel = jnp.where(mask, idx[:, None] - idx[None, :], 0.0)
    dmat = jnp.where(mask[None], jnp.exp(lg[:, None, None] * rel[None]), 0.0)
    cross = jnp.exp(lg[:, None] * (idx + 1.0)[None, :])
    upd = jnp.exp(lg[:, None] * (c - 1.0 - idx)[None, :])
    cdec = jnp.exp(lg * c)
    return dmat, cross, upd, cdec


def _level_masks():
    i = np.arange(CHUNK)[:, None]
    j = np.arange(CHUNK)[None, :]
    ms = [(i == j)]
    for s in LEVELS:
        ms.append((i // (2 * s) == j // (2 * s)) & (i % (2 * s) >= s) & (j % (2 * s) < s))
    return np.stack(ms).astype(np.float32)


def _level_factor(s, b, q, k, sub):
    def row(r):
        return jnp.broadcast_to(b[r:r + 1, :], (SUBLANES, LANES))

    refs, ws = [], []
    for m in range(CHUNK // SUBLANES):
        r0 = m * SUBLANES
        qb = q[r0:r0 + SUBLANES, :]
        kb = k[r0:r0 + SUBLANES, :]
        if s >= SUBLANES:
            blk = r0 // (2 * s)
            refs.append(row(blk * 2 * s + s - 1))
            ws.append(qb if (r0 % (2 * s)) >= s else kb)
        elif s == 4:
            refs.append(row(r0 + 3))
            ws.append(jnp.where(sub >= 4, qb, kb))
        elif s == 2:
            refs.append(jnp.where(sub < 4, row(r0 + 1), row(r0 + 5)))
            ws.append(jnp.where((sub & 3) >= 2, qb, kb))
        else:
            refs.append(jnp.where(sub < 2, row(r0),
                                  jnp.where(sub < 4, row(r0 + 2),
                                            jnp.where(sub < 6, row(r0 + 4), row(r0 + 6)))))
            ws.append(jnp.where((sub & 1) == 1, qb, kb))
    ref = jnp.concatenate(refs, axis=0)
    w = jnp.concatenate(ws, axis=0)
    return (w * jnp.exp(-jnp.abs(b - ref))).astype(BF16)


def _mix_prompt_kernel(x_ref, w_ref, lbl_ref, gnh_ref, gnr_ref, cos_ref, sin_ref, tril_ref,
                       masks_ref, dmat_ref, cross_ref, upd_ref, cdec_ref,
                       mix_ref, sh_ref, sr_ref,
                       qh, kh, vh, gh, gth, qr, kr, vr, gtr, sht):
    t = pl.program_id(1)

    @pl.when(t == 0)
    def _():
        sht[...] = jnp.zeros_like(sht)
        sr_ref[...] = jnp.zeros_like(sr_ref)

    lb = _lower_bound(lbl_ref)
    xb = x_ref[0].astype(BF16)
    _project_activate(xb, w_ref, lb, cos_ref[...], sin_ref[...],
                      qh, kh, vh, gh, gth, qr, kr, vr, gtr)

    sub = lax.broadcasted_iota(jnp.int32, (SUBLANES, LANES), 0)

    def chunk(ci, carry):
        rows = pl.ds(pl.multiple_of(ci * CHUNK, CHUNK), CHUNK)
        tril = tril_ref[...]
        hi, mid, lo = _split3(gh[rows, :])
        b_all = _dot(tril, hi) + (_dot(tril, mid) + _dot(tril, lo))
        for h in range(HEADS):
            hs = slice(h * HEAD_DIM, (h + 1) * HEAD_DIM)
            q = qh[rows, hs]
            k = kh[rows, hs]
            v = vh[rows, hs].astype(BF16)
            b = b_all[:, hs]
            b_last = b[CHUNK - 1:CHUNK, :]
            a = _dot_nt(q.astype(BF16), k.astype(BF16)) * masks_ref[0]
            for li, s in enumerate(LEVELS):
                xl = _level_factor(s, b, q, k, sub)
                a = a + _dot_nt(xl, xl) * masks_ref[li + 1]
            st = sht[h]
            o = _dot(a.astype(BF16), v) + _dot_nt((q * jnp.exp(b)).astype(BF16), st.astype(BF16))
            kd = (k * jnp.exp(b_last - b)).astype(BF16)
            sht[h] = st * jnp.exp(b_last) + _dot_tn(v, kd)
            mix_ref[0, rows, hs] = _head_norm_gate(o, gnh_ref[:, hs], gth[rows, hs]).astype(BF16)
            q = qr[rows, hs]
            k = kr[rows, hs]
            v = vr[rows, hs].astype(BF16)
            qb = q.astype(BF16)
            a = _dot_nt(qb, k.astype(BF16)) * dmat_ref[h]
            s_r = sr_ref[0, h]
            o = _dot(a.astype(BF16), v) + _dot(qb, s_r.astype(BF16)) * cross_ref[:, hs]
            ku = (k * upd_ref[:, hs]).astype(BF16)
            sr_ref[0, h] = s_r * cdec_ref[:, hs] + _dot_tn(ku, v)
            rs = slice(WIDTH + h * HEAD_DIM, WIDTH + (h + 1) * HEAD_DIM)
            mix_ref[0, rows, rs] = _head_norm_gate(o, gnr_ref[:, hs], gtr[rows, hs]).astype(BF16)
        return carry

    lax.fori_loop(0, PROMPT_TILE // CHUNK, chunk, 0)

    @pl.when(t == pl.num_programs(1) - 1)
    def _():
        for h in range(HEADS):
            sh_ref[0, h] = sht[h].T


def _const_spec(shape):
    nd = len(shape)
    return pl.BlockSpec(shape, lambda *_: (0,) * nd, pipeline_mode=pl.Buffered(1))


def _mix_prompt(x, w_in_b, lb_logits, gnh, gnr):
    bsz, seq, _ = x.shape
    nt = seq // PROMPT_TILE
    cos2, sin2 = _rope_tables(jnp.arange(seq, dtype=F32))
    dmat, cross, upd, cdec = _ret_consts(CHUNK)
    lane = lambda a: jnp.repeat(a, HEAD_DIM, axis=-1)
    cross_l = lane(cross.T)
    upd_l = lane(upd.T)
    cdec_l = lane(cdec[None, :])
    tril = jnp.asarray(np.tril(np.ones((CHUNK, CHUNK), np.float32)), BF16)
    masks = jnp.asarray(_level_masks())
    act = pltpu.VMEM((PROMPT_TILE, WIDTH), F32)
    state_shape = jax.ShapeDtypeStruct((bsz, HEADS, HEAD_DIM, HEAD_DIM), F32)
    state_spec = pl.BlockSpec((1, HEADS, HEAD_DIM, HEAD_DIM), lambda b, t: (b, 0, 0, 0))
    return pl.pallas_call(
        _mix_prompt_kernel,
        grid=(bsz, nt),
        in_specs=[
            pl.BlockSpec((1, PROMPT_TILE, D_MODEL), lambda b, t: (b, t, 0)),
            _const_spec((D_MODEL, IN_COLS)),
            _const_spec((2, WIDTH)),
            _const_spec((1, WIDTH)),
            _const_spec((1, WIDTH)),
            pl.BlockSpec((PROMPT_TILE, HEAD_DIM), lambda b, t: (t, 0)),
            pl.BlockSpec((PROMPT_TILE, HEAD_DIM), lambda b, t: (t, 0)),
            _const_spec((CHUNK, CHUNK)),
            _const_spec((len(LEVELS) + 1, CHUNK, CHUNK)),
            _const_spec((HEADS, CHUNK, CHUNK)),
            _const_spec((CHUNK, WIDTH)),
            _const_spec((CHUNK, WIDTH)),
            _const_spec((1, WIDTH)),
        ],
        out_specs=[
            pl.BlockSpec((1, PROMPT_TILE, 2 * WIDTH), lambda b, t: (b, t, 0)),
            state_spec,
            state_spec,
        ],
        out_shape=[jax.ShapeDtypeStruct((bsz, seq, 2 * WIDTH), BF16), state_shape, state_shape],
        scratch_shapes=[act] * 9 + [pltpu.VMEM((HEADS, HEAD_DIM, HEAD_DIM), F32)],
        compiler_params=pltpu.CompilerParams(
            dimension_semantics=("arbitrary", "arbitrary"), vmem_limit_bytes=VMEM_LIMIT),
        name="mix_prompt",
    )(x, w_in_b, lb_logits, gnh, gnr, cos2, sin2, tril, masks, dmat, cross_l, upd_l, cdec_l)


def _inproj_sample_kernel(x_ref, w_ref, lbl_ref, cos_ref, sin_ref,
                          qh, kh, vh, gh, gth, qr, kr, vr, gtr):
    lb = _lower_bound(lbl_ref)
    _project_activate(x_ref[...].astype(BF16), w_ref, lb, cos_ref[...], sin_ref[...],
                      qh, kh, vh, gh, gth, qr, kr, vr, gtr)


def _inproj_sample(x2d, w_in_b, lb_logits, cos2, sin2):
    n = x2d.shape[0]
    return pl.pallas_call(
        _inproj_sample_kernel,
        out_shape=(jax.ShapeDtypeStruct((n, WIDTH), F32),) * 9,
        compiler_params=pltpu.CompilerParams(vmem_limit_bytes=VMEM_LIMIT),
        name="inproj_sample",
    )(x2d, w_in_b, lb_logits, cos2, sin2)


def _mix_sample_kernel(rc_ref, qh, kh, vh, gh, gth, qr, kr, vr, gtr, shin, srin, gnh_ref, gnr_ref,
                       mix_ref, shout, srout, qd_s, kd_s, v_s, o_s):
    nseq = SAMPLE_SEQS
    nt = qh.shape[0]
    pad_rows = jnp.zeros((HEAD_DIM - nseq, HEAD_DIM), F32)

    def per_sequence(h, s_in, s_out, decay):
        for bb in range(nseq):
            seq_rows = pl.ds(bb, nt, stride=nseq)
            s = s_in[bb, h]
            o_s[seq_rows, :] = _dot(qd_s[seq_rows, :].astype(BF16), s.astype(BF16))
            s_out[bb, h] = decay(s, bb) + _dot_tn(kd_s[seq_rows, :].astype(BF16),
                                                  v_s[seq_rows, :].astype(BF16))

    for h in range(HEADS):
        hs = slice(h * HEAD_DIM, (h + 1) * HEAD_DIM)
        q = [qh[t, :, hs] for t in range(nt)]
        k = [kh[t, :, hs] for t in range(nt)]
        v = [vh[t, :, hs] for t in range(nt)]
        b = []
        for t in range(nt):
            g = gh[t, :, hs]
            b.append(g if t == 0 else b[-1] + g)
        b_last = b[-1]
        o = []
        for i in range(nt):
            acc = None
            for j in range(i + 1):
                w = q[i] * k[j]
                if j < i:
                    w = w * jnp.exp(b[i] - b[j])
                term = jnp.sum(w, axis=-1, keepdims=True) * v[j]
                acc = term if acc is None else acc + term
            o.append(acc)
        for t in range(nt):
            tr = slice(t * nseq, (t + 1) * nseq)
            qd_s[tr, :] = q[t] * jnp.exp(b[t])
            kd_s[tr, :] = k[t] * jnp.exp(b_last - b[t])
            v_s[tr, :] = v[t]
        dcol = jnp.concatenate([jnp.exp(b_last), pad_rows], axis=0).T
        per_sequence(h, shin, shout, lambda s, bb: s * dcol[:, bb:bb + 1])
        for t in range(nt):
            tr = slice(t * nseq, (t + 1) * nseq)
            mix_ref[t, :, hs] = _head_norm_gate(o[t] + o_s[tr, :], gnh_ref[:, hs],
                                                gth[t, :, hs]).astype(BF16)
        q = [qr[t, :, hs] for t in range(nt)]
        k = [kr[t, :, hs] for t in range(nt)]
        v = [vr[t, :, hs] for t in range(nt)]
        o = []
        for i in range(nt):
            acc = None
            for j in range(i + 1):
                a_ij = jnp.sum(q[i] * k[j], axis=-1, keepdims=True) * rc_ref[h, i * nt + j]
                term = a_ij * v[j]
                acc = term if acc is None else acc + term
            o.append(acc)
        for t in range(nt):
            tr = slice(t * nseq, (t + 1) * nseq)
            qd_s[tr, :] = q[t]
            kd_s[tr, :] = k[t] * rc_ref[h, nt * nt + nt + t]
            v_s[tr, :] = v[t]
        cdec = rc_ref[h, nt * nt + 2 * nt]
        per_sequence(h, srin, srout, lambda s, bb: s * cdec)
        rs = slice(WIDTH + h * HEAD_DIM, WIDTH + (h + 1) * HEAD_DIM)
        for t in range(nt):
            tr = slice(t * nseq, (t + 1) * nseq)
            cross_t = rc_ref[h, nt * nt + t]
            mix_ref[t, :, rs] = _head_norm_gate(o[t] + o_s[tr, :] * cross_t, gnr_ref[:, hs],
                                                gtr[t, :, hs]).astype(BF16)


def _mix_sample(acts, state_h, state_r, gnh, gnr, nt, nb):
    dmat, cross, upd, cdec = _ret_consts(nt)
    rc = jnp.concatenate([dmat.reshape(HEADS, nt * nt), cross, upd, cdec[:, None]], axis=1)
    acts3 = [a.reshape(nt, nb, WIDTH) for a in acts]
    act_spec = pl.BlockSpec((nt, SAMPLE_SEQS, WIDTH), lambda i: (0, i, 0))
    st_spec = pl.BlockSpec((SAMPLE_SEQS, HEADS, HEAD_DIM, HEAD_DIM), lambda i: (i, 0, 0, 0))
    st_shape = jax.ShapeDtypeStruct((nb, HEADS, HEAD_DIM, HEAD_DIM), F32)
    row_s = pltpu.VMEM((nt * SAMPLE_SEQS, HEAD_DIM), F32)
    return pl.pallas_call(
        _mix_sample_kernel,
        grid=(nb // SAMPLE_SEQS,),
        in_specs=[pl.BlockSpec(memory_space=pltpu.SMEM)] + [act_spec] * 9 + [st_spec, st_spec]
                 + [pl.BlockSpec((1, WIDTH), lambda i: (0, 0))] * 2,
        out_specs=[pl.BlockSpec((nt, SAMPLE_SEQS, 2 * WIDTH), lambda i: (0, i, 0)), st_spec, st_spec],
        out_shape=[jax.ShapeDtypeStruct((nt, nb, 2 * WIDTH), BF16), st_shape, st_shape],
        scratch_shapes=[row_s] * 4,
        compiler_params=pltpu.CompilerParams(
            dimension_semantics=("arbitrary",), vmem_limit_bytes=VMEM_LIMIT),
        name="mix_sample",
    )(rc, *acts3, state_h, state_r, gnh, gnr)


def _tail_kernel(mix_ref, x_ref, wo_ref, g1_ref, b1_ref, wg_ref, wu_ref, wd_ref, g2_ref, b2_ref,
                 y_ref):
    m = _dot(mix_ref[...], wo_ref[...])
    x1 = _layer_norm(DEEPNORM_ALPHA * x_ref[...] + m, g1_ref[...], b1_ref[...])
    xb = x1.astype(BF16)
    hidden = (_silu(_dot(xb, wg_ref[...])) * _dot(xb, wu_ref[...])).astype(BF16)
    d = _dot(hidden, wd_ref[...])
    y_ref[...] = _layer_norm(DEEPNORM_ALPHA * x1 + d, g2_ref[...], b2_ref[...])


def _tail(mix2d, x2d, wo, g1, b1, wg, wu, wd, g2, b2):
    n = x2d.shape[0]
    row_spec = lambda: pl.BlockSpec((TAIL_TILE, D_MODEL), lambda i: (i, 0))
    return pl.pallas_call(
        _tail_kernel,
        grid=(n // TAIL_TILE,),
        in_specs=[row_spec(), row_spec(),
                  _const_spec((D_MODEL, D_MODEL)), _const_spec((1, D_MODEL)), _const_spec((1, D_MODEL)),
                  _const_spec((D_MODEL, D_FF)), _const_spec((D_MODEL, D_FF)), _const_spec((D_FF, D_MODEL)),
                  _const_spec((1, D_MODEL)), _const_spec((1, D_MODEL))],
        out_specs=row_spec(),
        out_shape=jax.ShapeDtypeStruct((n, D_MODEL), F32),
        compiler_params=pltpu.CompilerParams(
            dimension_semantics=("arbitrary",), vmem_limit_bytes=VMEM_LIMIT),
        name="tail",
    )(mix2d, x2d, wo, g1, b1, wg, wu, wd, g2, b2)


def kernel(x_prompt, x_sample, state_hgrn, state_ret, w_in, lb_logits, hgrn_norm_g, ret_norm_g,
           w_out, ln1_g, ln1_b, w_gate, w_up, w_down, ln2_g, ln2_b):
    bsz, seq, _ = x_prompt.shape
    nb, nt, _ = x_sample.shape
    w_in_b = w_in[0].astype(BF16)
    tail_w = (w_out[0].astype(BF16), ln1_g, ln1_b, w_gate[0].astype(BF16), w_up[0].astype(BF16),
              w_down[0].astype(BF16), ln2_g, ln2_b)
    gnh = hgrn_norm_g[0].reshape(1, WIDTH)
    gnr = ret_norm_g[0].reshape(1, WIDTH)

    mix_p, sh_p, sr_p = _mix_prompt(x_prompt, w_in_b, lb_logits, gnh, gnr)
    y_p = _tail(mix_p.reshape(bsz * seq, 2 * WIDTH), x_prompt.reshape(bsz * seq, D_MODEL), *tail_w)

    xs = jnp.transpose(x_sample, (1, 0, 2)).reshape(nt * nb, D_MODEL)
    pos = jnp.repeat(jnp.arange(nt, dtype=F32) + float(PAST_LEN), nb)
    cos2, sin2 = _rope_tables(pos)
    acts = _inproj_sample(xs, w_in_b, lb_logits, cos2, sin2)
    mix_s, sh_s, sr_s = _mix_sample(acts, state_hgrn[0], state_ret[0], gnh, gnr, nt, nb)
    y_s = _tail(mix_s.reshape(nt * nb, 2 * WIDTH), xs, *tail_w)
    y_s = jnp.transpose(y_s.reshape(nt, nb, D_MODEL), (1, 0, 2))

    return (y_p.reshape(bsz, seq, D_MODEL), y_s, sh_p[None], sr_p[None], sh_s[None], sr_s[None])
```

```python
import functools
import math

import numpy as np
import jax
import jax.numpy as jnp
from jax import lax
from jax.experimental import pallas as pl
from jax.experimental.pallas import tpu as pltpu

F32 = jnp.float32
BF16 = jnp.bfloat16

D_MODEL = 1024
HEADS = 4
HEAD_DIM = 128
WIDTH = HEADS * HEAD_DIM
IN_COLS = 8 * WIDTH
D_FF = 2816
CHUNK = 128
ROPE_BASE = 10000.0
PAST_LEN = 16384
DEEPNORM_ALPHA = 2.0 ** 0.25
EPS = 1e-5
LOG2_E = 1.4426950408889634

LANES = 128
SUBLANES = 8
PROMPT_TILE = 512
TAIL_TILE = 512
SAMPLE_SEQS = 8
ROPE_TILE = 128
VMEM_LIMIT = 56 * 1024 * 1024

LEVELS = (64, 32, 16, 8, 4, 2, 1)


def _sigmoid_pair(x):
    e = jnp.exp(-jnp.abs(x))
    r = 1.0 / (1.0 + e)
    er = e * r
    pos = x >= 0
    return jnp.where(pos, r, er), jnp.where(pos, er, r)


def _silu(x):
    s, _ = _sigmoid_pair(x)
    return x * s


def _dot(a, b):
    return jnp.dot(a, b, preferred_element_type=F32)


def _dot_nt(a, b):
    return lax.dot_general(a, b, (((1,), (1,)), ((), ())), preferred_element_type=F32)


def _dot_tn(a, b):
    return lax.dot_general(a, b, (((0,), (0,)), ((), ())), preferred_element_type=F32)


def _split3(x):
    hi = x.astype(BF16)
    r = x - hi.astype(F32)
    mid = r.astype(BF16)
    lo = (r - mid.astype(F32)).astype(BF16)
    return hi, mid, lo


def _lower_bound(lbl_ref):
    l = lbl_ref[...]
    l0 = l[0:1, :]
    l1 = l[1:2, :]
    m = jnp.maximum(l0, l1)
    e0 = jnp.exp(l0 - m)
    e1 = jnp.exp(l1 - m)
    return e0 / (e0 + e1)


def _head_norm_gate(o, gain, gate):
    ms = jnp.mean(o * o, axis=-1, keepdims=True)
    return o * lax.rsqrt(ms + EPS) * gain * gate


def _layer_norm(h, g, b):
    mu = jnp.mean(h, axis=-1, keepdims=True)
    d = h - mu
    var = jnp.mean(d * d, axis=-1, keepdims=True)
    return d * lax.rsqrt(var + EPS) * g + b


def _project_activate(xb, w_ref, lb, cos2, sin2, qh, kh, vh, gh, gth, qr, kr, vr, gtr):
    def proj(g):
        return _dot(xb, w_ref[:, g * WIDTH:(g + 1) * WIDTH])

    qh[...] = _silu(proj(0))
    s_pos, s_neg = _sigmoid_pair(proj(1))
    gh[...] = jnp.log(lb + (1.0 - lb) * s_pos) * LOG2_E
    kh[...] = (1.0 - lb) * s_neg
    vh[...] = proj(2)
    gth[...] = _silu(proj(3))

    def rotary(p, scale):
        for h in range(HEADS):
            hs = slice(h * HEAD_DIM, (h + 1) * HEAD_DIM)
            xh = p[:, hs]
            r = xh * cos2 + pltpu.roll(xh, HEAD_DIM // 2, 1) * sin2
            yield hs, (r if scale is None else r * scale)

    for hs, r in rotary(proj(4), None):
        qr[:, hs] = r
    for hs, r in rotary(proj(5), HEAD_DIM ** -0.5):
        kr[:, hs] = r
    vr[...] = proj(6)
    gtr[...] = _silu(proj(7))


def _rope_kernel(pos_ref, inv_ref, sign_ref, cos_ref, sin_ref):
    ang = pos_ref[...] * inv_ref[...]
    cos_ref[...] = jnp.cos(ang)
    sin_ref[...] = jnp.sin(ang) * sign_ref[...]


def _rope_tables(pos):
    n = pos.shape[0]
    half = HEAD_DIM // 2
    inv = ROPE_BASE ** (-jnp.arange(half, dtype=F32) / half)
    inv2 = jnp.concatenate([inv, inv])[None, :]
    sign = jnp.asarray(np.concatenate([-np.ones(half), np.ones(half)])[None, :], F32)
    tbl = pl.BlockSpec((ROPE_TILE, HEAD_DIM), lambda i: (i, 0))
    row = pl.BlockSpec((1, HEAD_DIM), lambda i: (0, 0))
    return pl.pallas_call(
        _rope_kernel,
        grid=(n // ROPE_TILE,),
        in_specs=[pl.BlockSpec((ROPE_TILE, 1), lambda i: (i, 0)), row, row],
        out_specs=[tbl, tbl],
        out_shape=(jax.ShapeDtypeStruct((n, HEAD_DIM), F32),) * 2,
        name="rope_tables",
    )(pos[:, None], inv2, sign)


def _ret_consts(c):
    lg = jnp.log1p(-jnp.exp2(-5.0 - jnp.arange(HEADS, dtype=F32)))
    idx = jnp.arange(c, dtype=F32)
    mask = jnp.tril(jnp.ones((c, c), dtype=bool))
    rel = jnp.where(mask, idx[:, None] - idx[None, :], 0.0)
    dmat = jnp.where(mask[None], jnp.exp(lg[:, None, None] * rel[None]), 0.0)
    cross = jnp.exp(lg[:, None] * (idx + 1.0)[None, :])
    upd = jnp.exp(lg[:, None] * (c - 1.0 - idx)[None, :])
    cdec = jnp.exp(lg * c)
    return dmat, cross, upd, cdec


def _level_masks():
    i = np.arange(CHUNK)[:, None]
    j = np.arange(CHUNK)[None, :]
    ms = [(i == j)]
    for s in LEVELS:
        ms.append((i // (2 * s) == j // (2 * s)) & (i % (2 * s) >= s) & (j % (2 * s) < s))
    return np.stack(ms).astype(np.float32)


def _level_factor(s, b, q, k, sub):
    def row(r):
        return jnp.broadcast_to(b[r:r + 1, :], (SUBLANES, LANES))

    es, ws = [], []
    for m in range(CHUNK // SUBLANES):
        r0 = m * SUBLANES
        qb = q[r0:r0 + SUBLANES, :]
        kb = k[r0:r0 + SUBLANES, :]
        bb = b[r0:r0 + SUBLANES, :]
        if s >= SUBLANES:
            ref = row((r0 // (2 * s)) * 2 * s + s - 1)
            lower = (r0 % (2 * s)) >= s
            es.append(bb - ref if lower else ref - bb)
            ws.append(qb if lower else kb)
        elif s == 4:
            es.append(-jnp.abs(bb - row(r0 + 3)))
            ws.append(jnp.where(sub >= 4, qb, kb))
        elif s == 2:
            es.append(-jnp.abs(bb - jnp.where(sub < 4, row(r0 + 1), row(r0 + 5))))
            ws.append(jnp.where((sub & 3) >= 2, qb, kb))
        else:
            ref = jnp.where(sub < 2, row(r0),
                            jnp.where(sub < 4, row(r0 + 2), jnp.where(sub < 6, row(r0 + 4), row(r0 + 6))))
            es.append(-jnp.abs(bb - ref))
            ws.append(jnp.where((sub & 1) == 1, qb, kb))
    e = jnp.concatenate(es, axis=0)
    w = jnp.concatenate(ws, axis=0)
    return (w * jnp.exp2(e)).astype(BF16)


def _mix_chunks(acts, sht, sr_ref, mix_ref, gnh_ref, gnr_ref, tril_ref, masks_ref, dmat_ref,
                cross_ref, upd_ref, cdec_ref):
    qh, kh, vh, gh, gth, qr, kr, vr, gtr = acts
    sub = lax.broadcasted_iota(jnp.int32, (SUBLANES, LANES), 0)
    for ci in range(PROMPT_TILE // CHUNK):
        rows = slice(ci * CHUNK, (ci + 1) * CHUNK)
        b_all = _dot(tril_ref[...], jnp.concatenate(_split3(gh[rows, :]), axis=0))
        for h in range(HEADS):
            hs = slice(h * HEAD_DIM, (h + 1) * HEAD_DIM)
            q = qh[rows, hs]
            k = kh[rows, hs]
            v = vh[rows, hs].astype(BF16)
            b = b_all[:, hs]
            b_last = b[CHUNK - 1:CHUNK, :]
            a = _dot_nt(q.astype(BF16), k.astype(BF16)) * masks_ref[0]
            for li, s in enumerate(LEVELS):
                xl = _level_factor(s, b, q, k, sub)
                a = a + _dot_nt(xl, xl) * masks_ref[li + 1]
            st = sht[h]
            o = _dot(a.astype(BF16), v) + _dot_nt((q * jnp.exp2(b)).astype(BF16), st.astype(BF16))
            kd = (k * jnp.exp2(b_last - b)).astype(BF16)
            sht[h] = st * jnp.exp2(b_last) + _dot_tn(v, kd)
            mix_ref[0, rows, hs] = _head_norm_gate(o, gnh_ref[:, hs], gth[rows, hs]).astype(BF16)
            q = qr[rows, hs]
            k = kr[rows, hs]
            v = vr[rows, hs].astype(BF16)
            a = _dot_nt(q.astype(BF16), k.astype(BF16)) * dmat_ref[h]
            s_r = sr_ref[0, h]
            qa = jnp.concatenate([(q * cross_ref[:, hs]).astype(BF16), a.astype(BF16)], axis=1)
            o = _dot(qa, jnp.concatenate([s_r.astype(BF16), v], axis=0))
            ku = (k * upd_ref[:, hs]).astype(BF16)
            sr_ref[0, h] = s_r * cdec_ref[:, hs] + _dot_tn(ku, v)
            rs = slice(WIDTH + h * HEAD_DIM, WIDTH + (h + 1) * HEAD_DIM)
            mix_ref[0, rows, rs] = _head_norm_gate(o, gnr_ref[:, hs], gtr[rows, hs]).astype(BF16)


def _mix_prompt_kernel(x_ref, w_ref, lbl_ref, gnh_ref, gnr_ref, cos_ref, sin_ref,
                       tril_ref, masks_ref, dmat_ref, cross_ref, upd_ref, cdec_ref,
                       mix_ref, sh_ref, sr_ref, *scratch):
    acts, sht = scratch[:9], scratch[9]
    t = pl.program_id(1)

    @pl.when(t == 0)
    def _():
        sht[...] = jnp.zeros_like(sht)
        sr_ref[...] = jnp.zeros_like(sr_ref)

    lb = _lower_bound(lbl_ref)
    _project_activate(x_ref[0].astype(BF16), w_ref, lb, cos_ref[...], sin_ref[...], *acts)
    _mix_chunks(acts, sht, sr_ref, mix_ref, gnh_ref, gnr_ref, tril_ref, masks_ref,
                dmat_ref, cross_ref, upd_ref, cdec_ref)

    @pl.when(t == pl.num_programs(1) - 1)
    def _():
        for h in range(HEADS):
            sh_ref[0, h] = sht[h].T


def _const_spec(shape):
    nd = len(shape)
    return pl.BlockSpec(shape, lambda *_: (0,) * nd, pipeline_mode=pl.Buffered(1))


def _mix_prompt(x, w_in_b, lb_logits, gnh, gnr):
    bsz, seq, _ = x.shape
    nt = seq // PROMPT_TILE
    cos2, sin2 = _rope_tables(jnp.arange(seq, dtype=F32))
    dmat, cross, upd, cdec = _ret_consts(CHUNK)
    lane = lambda a: jnp.repeat(a, HEAD_DIM, axis=-1)
    cross_l = lane(cross.T)
    upd_l = lane(upd.T)
    cdec_l = lane(cdec[None, :])
    tril = np.tril(np.ones((CHUNK, CHUNK), np.float32))
    tril3 = jnp.asarray(np.concatenate([tril] * 3, axis=1), BF16)
    masks = jnp.asarray(_level_masks())
    act = pltpu.VMEM((PROMPT_TILE, WIDTH), F32)
    state_shape = jax.ShapeDtypeStruct((bsz, HEADS, HEAD_DIM, HEAD_DIM), F32)
    state_spec = pl.BlockSpec((1, HEADS, HEAD_DIM, HEAD_DIM), lambda b, t: (b, 0, 0, 0))
    rope_spec = pl.BlockSpec((PROMPT_TILE, HEAD_DIM), lambda b, t: (t, 0))
    return pl.pallas_call(
        _mix_prompt_kernel,
        grid=(bsz, nt),
        in_specs=[
            pl.BlockSpec((1, PROMPT_TILE, D_MODEL), lambda b, t: (b, t, 0)),
            _const_spec((D_MODEL, IN_COLS)),
            _const_spec((2, WIDTH)),
            _const_spec((1, WIDTH)),
            _const_spec((1, WIDTH)),
            rope_spec,
            rope_spec,
            _const_spec((CHUNK, 3 * CHUNK)),
            _const_spec((len(LEVELS) + 1, CHUNK, CHUNK)),
            _const_spec((HEADS, CHUNK, CHUNK)),
            _const_spec((CHUNK, WIDTH)),
            _const_spec((CHUNK, WIDTH)),
            _const_spec((1, WIDTH)),
        ],
        out_specs=[
            pl.BlockSpec((1, PROMPT_TILE, 2 * WIDTH), lambda b, t: (b, t, 0)),
            state_spec,
            state_spec,
        ],
        out_shape=[jax.ShapeDtypeStruct((bsz, seq, 2 * WIDTH), BF16), state_shape, state_shape],
        scratch_shapes=[act] * 9 + [pltpu.VMEM((HEADS, HEAD_DIM, HEAD_DIM), F32)],
        compiler_params=pltpu.CompilerParams(
            dimension_semantics=("arbitrary", "arbitrary"), vmem_limit_bytes=VMEM_LIMIT),
        name="mix_prompt",
    )(x, w_in_b, lb_logits, gnh, gnr, cos2, sin2, tril3, masks, dmat, cross_l, upd_l, cdec_l)


def _inproj_sample_kernel(x_ref, w_ref, lbl_ref, cos_ref, sin_ref,
                          qh, kh, vh, gh, gth, qr, kr, vr, gtr):
    lb = _lower_bound(lbl_ref)
    _project_activate(x_ref[...].astype(BF16), w_ref, lb, cos_ref[...], sin_ref[...],
                      qh, kh, vh, gh, gth, qr, kr, vr, gtr)


def _inproj_sample(x2d, w_in_b, lb_logits, cos2, sin2):
    n = x2d.shape[0]
    return pl.pallas_call(
        _inproj_sample_kernel,
        out_shape=(jax.ShapeDtypeStruct((n, WIDTH), F32),) * 9,
        compiler_params=pltpu.CompilerParams(vmem_limit_bytes=VMEM_LIMIT),
        name="inproj_sample",
    )(x2d, w_in_b, lb_logits, cos2, sin2)


def _mix_sample_kernel(rc_ref, qh, kh, vh, gh, gth, qr, kr, vr, gtr, shin, srin, gnh_ref, gnr_ref,
                       mix_ref, shout, srout, qd_s, kd_s, v_s, o_s):
    nseq = SAMPLE_SEQS
    nt = qh.shape[0]
    pad_rows = jnp.zeros((HEAD_DIM - nseq, HEAD_DIM), F32)

    def per_sequence(h, s_in, s_out, decay):
        for bb in range(nseq):
            seq_rows = pl.ds(bb, nt, stride=nseq)
            s = s_in[bb, h]
            o_s[seq_rows, :] = _dot(qd_s[seq_rows, :].astype(BF16), s.astype(BF16))
            s_out[bb, h] = decay(s, bb) + _dot_tn(kd_s[seq_rows, :].astype(BF16),
                                                  v_s[seq_rows, :].astype(BF16))

    for h in range(HEADS):
        hs = slice(h * HEAD_DIM, (h + 1) * HEAD_DIM)
        q = [qh[t, :, hs] for t in range(nt)]
        k = [kh[t, :, hs] for t in range(nt)]
        v = [vh[t, :, hs] for t in range(nt)]
        b = []
        for t in range(nt):
            g = gh[t, :, hs]
            b.append(g if t == 0 else b[-1] + g)
        b_last = b[-1]
        o = []
        for i in range(nt):
            acc = None
            for j in range(i + 1):
                w = q[i] * k[j]
                if j < i:
                    w = w * jnp.exp2(b[i] - b[j])
                term = jnp.sum(w, axis=-1, keepdims=True) * v[j]
                acc = term if acc is None else acc + term
            o.append(acc)
        for t in range(nt):
            tr = slice(t * nseq, (t + 1) * nseq)
            qd_s[tr, :] = q[t] * jnp.exp2(b[t])
            kd_s[tr, :] = k[t] * jnp.exp2(b_last - b[t])
            v_s[tr, :] = v[t]
        dcol = jnp.concatenate([jnp.exp2(b_last), pad_rows], axis=0).T
        per_sequence(h, shin, shout, lambda s, bb: s * dcol[:, bb:bb + 1])
        for t in range(nt):
            tr = slice(t * nseq, (t + 1) * nseq)
            mix_ref[t, :, hs] = _head_norm_gate(o[t] + o_s[tr, :], gnh_ref[:, hs],
                                                gth[t, :, hs]).astype(BF16)
        q = [qr[t, :, hs] for t in range(nt)]
        k = [kr[t, :, hs] for t in range(nt)]
        v = [vr[t, :, hs] for t in range(nt)]
        o = []
        for i in range(nt):
            acc = None
            for j in range(i + 1):
                a_ij = jnp.sum(q[i] * k[j], axis=-1, keepdims=True) * rc_ref[h, i * nt + j]
                term = a_ij * v[j]
                acc = term if acc is None else acc + term
            o.append(acc)
        for t in range(nt):
            tr = slice(t * nseq, (t + 1) * nseq)
            qd_s[tr, :] = q[t]
            kd_s[tr, :] = k[t] * rc_ref[h, nt * nt + nt + t]
            v_s[tr, :] = v[t]
        cdec = rc_ref[h, nt * nt + 2 * nt]
        per_sequence(h, srin, srout, lambda s, bb: s * cdec)
        rs = slice(WIDTH + h * HEAD_DIM, WIDTH + (h + 1) * HEAD_DIM)
        for t in range(nt):
            tr = slice(t * nseq, (t + 1) * nseq)
            cross_t = rc_ref[h, nt * nt + t]
            mix_ref[t, :, rs] = _head_norm_gate(o[t] + o_s[tr, :] * cross_t, gnr_ref[:, hs],
                                                gtr[t, :, hs]).astype(BF16)


def _mix_sample(acts, state_h, state_r, gnh, gnr, nt, nb):
    dmat, cross, upd, cdec = _ret_consts(nt)
    rc = jnp.concatenate([dmat.reshape(HEADS, nt * nt), cross, upd, cdec[:, None]], axis=1)
    acts3 = [a.reshape(nt, nb, WIDTH) for a in acts]
    act_spec = pl.BlockSpec((nt, SAMPLE_SEQS, WIDTH), lambda i: (0, i, 0))
    st_spec = pl.BlockSpec((SAMPLE_SEQS, HEADS, HEAD_DIM, HEAD_DIM), lambda i: (i, 0, 0, 0))
    st_shape = jax.ShapeDtypeStruct((nb, HEADS, HEAD_DIM, HEAD_DIM), F32)
    row_s = pltpu.VMEM((nt * SAMPLE_SEQS, HEAD_DIM), F32)
    return pl.pallas_call(
        _mix_sample_kernel,
        grid=(nb // SAMPLE_SEQS,),
        in_specs=[pl.BlockSpec(memory_space=pltpu.SMEM)] + [act_spec] * 9 + [st_spec, st_spec]
                 + [pl.BlockSpec((1, WIDTH), lambda i: (0, 0))] * 2,
        out_specs=[pl.BlockSpec((nt, SAMPLE_SEQS, 2 * WIDTH), lambda i: (0, i, 0)), st_spec, st_spec],
        out_shape=[jax.ShapeDtypeStruct((nt, nb, 2 * WIDTH), BF16), st_shape, st_shape],
        scratch_shapes=[row_s] * 4,
        compiler_params=pltpu.CompilerParams(
            dimension_semantics=("arbitrary",), vmem_limit_bytes=VMEM_LIMIT),
        name="mix_sample",
    )(rc, *acts3, state_h, state_r, gnh, gnr)


def _tail_kernel(mix_ref, x_ref, wo_ref, g1_ref, b1_ref, wg_ref, wu_ref, wd_ref, g2_ref, b2_ref,
                 y_ref):
    m = _dot(mix_ref[...], wo_ref[...])
    x1 = _layer_norm(DEEPNORM_ALPHA * x_ref[...] + m, g1_ref[...], b1_ref[...])
    xb = x1.astype(BF16)
    hidden = (_silu(_dot(xb, wg_ref[...])) * _dot(xb, wu_ref[...])).astype(BF16)
    d = _dot(hidden, wd_ref[...])
    y_ref[...] = _layer_norm(DEEPNORM_ALPHA * x1 + d, g2_ref[...], b2_ref[...])


def _tail(mix2d, x2d, wo, g1, b1, wg, wu, wd, g2, b2):
    n = x2d.shape[0]
    row_spec = lambda: pl.BlockSpec((TAIL_TILE, D_MODEL), lambda i: (i, 0))
    return pl.pallas_call(
        _tail_kernel,
        grid=(n // TAIL_TILE,),
        in_specs=[row_spec(), row_spec(),
                  _const_spec((D_MODEL, D_MODEL)), _const_spec((1, D_MODEL)), _const_spec((1, D_MODEL)),
                  _const_spec((D_MODEL, D_FF)), _const_spec((D_MODEL, D_FF)), _const_spec((D_FF, D_MODEL)),
                  _const_spec((1, D_MODEL)), _const_spec((1, D_MODEL))],
        out_specs=row_spec(),
        out_shape=jax.ShapeDtypeStruct((n, D_MODEL), F32),
        compiler_params=pltpu.CompilerParams(
            dimension_semantics=("arbitrary",), vmem_limit_bytes=VMEM_LIMIT),
        name="tail",
    )(mix2d, x2d, wo, g1, b1, wg, wu, wd, g2, b2)


def kernel(x_prompt, x_sample, state_hgrn, state_ret, w_in, lb_logits, hgrn_norm_g, ret_norm_g,
           w_out, ln1_g, ln1_b, w_gate, w_up, w_down, ln2_g, ln2_b):
    bsz, seq, _ = x_prompt.shape
    nb, nt, _ = x_sample.shape
    w_in_b = w_in[0].astype(BF16)
    tail_w = (w_out[0].astype(BF16), ln1_g, ln1_b, w_gate[0].astype(BF16), w_up[0].astype(BF16),
              w_down[0].astype(BF16), ln2_g, ln2_b)
    gnh = hgrn_norm_g[0].reshape(1, WIDTH)
    gnr = ret_norm_g[0].reshape(1, WIDTH)

    mix_p, sh_p, sr_p = _mix_prompt(x_prompt, w_in_b, lb_logits, gnh, gnr)
    y_p = _tail(mix_p.reshape(bsz * seq, 2 * WIDTH), x_prompt.reshape(bsz * seq, D_MODEL), *tail_w)

    xs = jnp.transpose(x_sample, (1, 0, 2)).reshape(nt * nb, D_MODEL)
    pos = jnp.repeat(jnp.arange(nt, dtype=F32) + float(PAST_LEN), nb)
    cos2, sin2 = _rope_tables(pos)
    acts = _inproj_sample(xs, w_in_b, lb_logits, cos2, sin2)
    mix_s, sh_s, sr_s = _mix_sample(acts, state_hgrn[0], state_ret[0], gnh, gnr, nt, nb)
    y_s = _tail(mix_s.reshape(nt * nb, 2 * WIDTH), xs, *tail_w)
    y_s = jnp.transpose(y_s.reshape(nt, nb, D_MODEL), (1, 0, 2))

    return (y_p.reshape(bsz, seq, D_MODEL), y_s, sh_p[None], sr_p[None], sh_s[None], sr_s[None])
```

```python
import functools
import math

import numpy as np
import jax
import jax.numpy as jnp
from jax import lax
from jax.experimental import pallas as pl
from jax.experimental.pallas import tpu as pltpu

F32 = jnp.float32
BF16 = jnp.bfloat16

D_MODEL = 1024
HEADS = 4
HEAD_DIM = 128
WIDTH = HEADS * HEAD_DIM
IN_COLS = 8 * WIDTH
D_FF = 2816
CHUNK = 128
ROPE_BASE = 10000.0
PAST_LEN = 16384
DEEPNORM_ALPHA = 2.0 ** 0.25
EPS = 1e-5
LOG2_E = 1.4426950408889634

LANES = 128
SUBLANES = 8
PROMPT_TILE = 512
TAIL_TILE = 512
SAMPLE_SEQS = 8
ROPE_TILE = 128
VMEM_LIMIT = 56 * 1024 * 1024

LEVELS = (64, 32, 16, 8, 4, 2, 1)
FAST_BLOCK = 64
FAST_SPAN_BITS = 100.0
FAST_Q_MAX = 2.0 ** 20


def _sigmoid_pair(x):
    e = jnp.exp(-jnp.abs(x))
    r = 1.0 / (1.0 + e)
    er = e * r
    pos = x >= 0
    return jnp.where(pos, r, er), jnp.where(pos, er, r)


def _silu(x):
    s, _ = _sigmoid_pair(x)
    return x * s


def _dot(a, b):
    return jnp.dot(a, b, preferred_element_type=F32)


def _dot_nt(a, b):
    return lax.dot_general(a, b, (((1,), (1,)), ((), ())), preferred_element_type=F32)


def _dot_tn(a, b):
    return lax.dot_general(a, b, (((0,), (0,)), ((), ())), preferred_element_type=F32)


def _split3(x):
    hi = x.astype(BF16)
    r = x - hi.astype(F32)
    mid = r.astype(BF16)
    lo = (r - mid.astype(F32)).astype(BF16)
    return hi, mid, lo


def _lower_bound(lbl_ref):
    l = lbl_ref[...]
    l0 = l[0:1, :]
    l1 = l[1:2, :]
    m = jnp.maximum(l0, l1)
    e0 = jnp.exp(l0 - m)
    e1 = jnp.exp(l1 - m)
    return e0 / (e0 + e1)


def _head_norm_gate(o, gain, gate):
    ms = jnp.mean(o * o, axis=-1, keepdims=True)
    return o * lax.rsqrt(ms + EPS) * gain * gate


def _layer_norm(h, g, b):
    mu = jnp.mean(h, axis=-1, keepdims=True)
    d = h - mu
    var = jnp.mean(d * d, axis=-1, keepdims=True)
    return d * lax.rsqrt(var + EPS) * g + b


def _project_activate(xb, w_ref, lb, cos2, sin2, qh, kh, vh, gh, gth, qr, kr, vr, gtr):
    def proj(g):
        return _dot(xb, w_ref[:, g * WIDTH:(g + 1) * WIDTH])

    qh[...] = _silu(proj(0))
    s_pos, s_neg = _sigmoid_pair(proj(1))
    gh[...] = jnp.log(lb + (1.0 - lb) * s_pos) * LOG2_E
    kh[...] = (1.0 - lb) * s_neg
    vh[...] = proj(2)
    gth[...] = _silu(proj(3))

    def rotary(p, scale):
        for h in range(HEADS):
            hs = slice(h * HEAD_DIM, (h + 1) * HEAD_DIM)
            xh = p[:, hs]
            r = xh * cos2 + pltpu.roll(xh, HEAD_DIM // 2, 1) * sin2
            yield hs, (r if scale is None else r * scale)

    for hs, r in rotary(proj(4), None):
        qr[:, hs] = r
    for hs, r in rotary(proj(5), HEAD_DIM ** -0.5):
        kr[:, hs] = r
    vr[...] = proj(6)
    gtr[...] = _silu(proj(7))


def _rope_kernel(pos_ref, inv_ref, sign_ref, cos_ref, sin_ref):
    ang = pos_ref[...] * inv_ref[...]
    cos_ref[...] = jnp.cos(ang)
    sin_ref[...] = jnp.sin(ang) * sign_ref[...]


def _rope_tables(pos):
    n = pos.shape[0]
    half = HEAD_DIM // 2
    inv = ROPE_BASE ** (-jnp.arange(half, dtype=F32) / half)
    inv2 = jnp.concatenate([inv, inv])[None, :]
    sign = jnp.asarray(np.concatenate([-np.ones(half), np.ones(half)])[None, :], F32)
    tbl = pl.BlockSpec((ROPE_TILE, HEAD_DIM), lambda i: (i, 0))
    row = pl.BlockSpec((1, HEAD_DIM), lambda i: (0, 0))
    return pl.pallas_call(
        _rope_kernel,
        grid=(n // ROPE_TILE,),
        in_specs=[pl.BlockSpec((ROPE_TILE, 1), lambda i: (i, 0)), row, row],
        out_specs=[tbl, tbl],
        out_shape=(jax.ShapeDtypeStruct((n, HEAD_DIM), F32),) * 2,
        name="rope_tables",
    )(pos[:, None], inv2, sign)


def _ret_consts(c):
    lg = jnp.log1p(-jnp.exp2(-5.0 - jnp.arange(HEADS, dtype=F32)))
    idx = jnp.arange(c, dtype=F32)
    mask = jnp.tril(jnp.ones((c, c), dtype=bool))
    rel = jnp.where(mask, idx[:, None] - idx[None, :], 0.0)
    dmat = jnp.where(mask[None], jnp.exp(lg[:, None, None] * rel[None]), 0.0)
    cross = jnp.exp(lg[:, None] * (idx + 1.0)[None, :])
    upd = jnp.exp(lg[:, None] * (c - 1.0 - idx)[None, :])
    cdec = jnp.exp(lg * c)
    return dmat, cross, upd, cdec


def _level_masks():
    i = np.arange(CHUNK)[:, None]
    j = np.arange(CHUNK)[None, :]
    ms = [(i == j)]
    for s in LEVELS:
        ms.append((i // (2 * s) == j // (2 * s)) & (i % (2 * s) >= s) & (j % (2 * s) < s))
    ms.append((i // FAST_BLOCK == j // FAST_BLOCK) & (i >= j))
    return np.stack(ms).astype(np.float32)


def _level_factor(s, b, q, k, sub):
    def row(r):
        return jnp.broadcast_to(b[r:r + 1, :], (SUBLANES, LANES))

    es, ws = [], []
    for m in range(CHUNK // SUBLANES):
        r0 = m * SUBLANES
        qb = q[r0:r0 + SUBLANES, :]
        kb = k[r0:r0 + SUBLANES, :]
        bb = b[r0:r0 + SUBLANES, :]
        if s >= SUBLANES:
            ref = row((r0 // (2 * s)) * 2 * s + s - 1)
            lower = (r0 % (2 * s)) >= s
            es.append(bb - ref if lower else ref - bb)
            ws.append(qb if lower else kb)
        elif s == 4:
            es.append(-jnp.abs(bb - row(r0 + 3)))
            ws.append(jnp.where(sub >= 4, qb, kb))
        elif s == 2:
            es.append(-jnp.abs(bb - jnp.where(sub < 4, row(r0 + 1), row(r0 + 5))))
            ws.append(jnp.where((sub & 3) >= 2, qb, kb))
        else:
            ref = jnp.where(sub < 2, row(r0),
                            jnp.where(sub < 4, row(r0 + 2), jnp.where(sub < 6, row(r0 + 4), row(r0 + 6))))
            es.append(-jnp.abs(bb - ref))
            ws.append(jnp.where((sub & 1) == 1, qb, kb))
    e = jnp.concatenate(es, axis=0)
    w = jnp.concatenate(ws, axis=0)
    return (w * jnp.exp2(e)).astype(BF16)


def _fast_block_factors(b, q, k):
    refs = []
    for blk in range(CHUNK // FAST_BLOCK):
        r = blk * FAST_BLOCK + FAST_BLOCK // 2 - 1
        refs.append(jnp.broadcast_to(b[r:r + 1, :], (FAST_BLOCK, LANES)))
    e = b - jnp.concatenate(refs, axis=0)
    return (q * jnp.exp2(e)).astype(BF16), (k * jnp.exp2(-e)).astype(BF16)


def _mix_chunks(fast, acts, b_s, sht, sr_ref, mix_ref, gnh_ref, gnr_ref, masks_ref, dmat_ref,
                cross_ref, upd_ref, cdec_ref):
    qh, kh, vh, _, gth, qr, kr, vr, gtr = acts
    sub = lax.broadcasted_iota(jnp.int32, (SUBLANES, LANES), 0)
    for ci in range(PROMPT_TILE // CHUNK):
        rows = slice(ci * CHUNK, (ci + 1) * CHUNK)
        for h in range(HEADS):
            hs = slice(h * HEAD_DIM, (h + 1) * HEAD_DIM)
            q = qh[rows, hs]
            k = kh[rows, hs]
            v = vh[rows, hs].astype(BF16)
            b = b_s[rows, hs]
            b_last = b[CHUNK - 1:CHUNK, :]
            if fast:
                qf, kf = _fast_block_factors(b, q, k)
                a = jnp.where(masks_ref[len(LEVELS) + 1] > 0.5, _dot_nt(qf, kf), 0.0)
            else:
                a = _dot_nt(q.astype(BF16), k.astype(BF16)) * masks_ref[0]
            for li, s in enumerate(LEVELS):
                if fast and s < FAST_BLOCK:
                    continue
                xl = _level_factor(s, b, q, k, sub)
                a = a + _dot_nt(xl, xl) * masks_ref[li + 1]
            st = sht[h]
            o = _dot(a.astype(BF16), v) + _dot_nt((q * jnp.exp2(b)).astype(BF16), st.astype(BF16))
            kd = (k * jnp.exp2(b_last - b)).astype(BF16)
            sht[h] = st * jnp.exp2(b_last) + _dot_tn(v, kd)
            mix_ref[0, rows, hs] = _head_norm_gate(o, gnh_ref[:, hs], gth[rows, hs]).astype(BF16)
            q = qr[rows, hs]
            k = kr[rows, hs]
            v = vr[rows, hs].astype(BF16)
            a = _dot_nt(q.astype(BF16), k.astype(BF16)) * dmat_ref[h]
            s_r = sr_ref[0, h]
            qa = jnp.concatenate([(q * cross_ref[:, hs]).astype(BF16), a.astype(BF16)], axis=1)
            o = _dot(qa, jnp.concatenate([s_r.astype(BF16), v], axis=0))
            ku = (k * upd_ref[:, hs]).astype(BF16)
            sr_ref[0, h] = s_r * cdec_ref[:, hs] + _dot_tn(ku, v)
            rs = slice(WIDTH + h * HEAD_DIM, WIDTH + (h + 1) * HEAD_DIM)
            mix_ref[0, rows, rs] = _head_norm_gate(o, gnr_ref[:, hs], gtr[rows, hs]).astype(BF16)


def _mix_prompt_kernel(x_ref, w_ref, lbl_ref, gnh_ref, gnr_ref, cos_ref, sin_ref,
                       tril_ref, masks_ref, dmat_ref, cross_ref, upd_ref, cdec_ref,
                       mix_ref, sh_ref, sr_ref, *scratch):
    acts, b_s, sht = scratch[:9], scratch[9], scratch[10]
    t = pl.program_id(1)

    @pl.when(t == 0)
    def _():
        sht[...] = jnp.zeros_like(sht)
        sr_ref[...] = jnp.zeros_like(sr_ref)

    lb = _lower_bound(lbl_ref)
    _project_activate(x_ref[0].astype(BF16), w_ref, lb, cos_ref[...], sin_ref[...], *acts)

    qh, gh = acts[0], acts[3]
    span = None
    for ci in range(PROMPT_TILE // CHUNK):
        rows = slice(ci * CHUNK, (ci + 1) * CHUNK)
        b_all = _dot(tril_ref[...], jnp.concatenate(_split3(gh[rows, :]), axis=0))
        b_s[rows, :] = b_all
        prev = None
        for r in range(FAST_BLOCK // 2 - 1, CHUNK, FAST_BLOCK // 2):
            cur = b_all[r:r + 1, :]
            d = -cur if prev is None else prev - cur
            span = d if span is None else jnp.maximum(span, d)
            prev = cur
    fast = (jnp.max(span) < FAST_SPAN_BITS) & (jnp.max(jnp.abs(qh[...])) < FAST_Q_MAX)
    mix_args = (acts, b_s, sht, sr_ref, mix_ref, gnh_ref, gnr_ref, masks_ref, dmat_ref, cross_ref,
                upd_ref, cdec_ref)

    @pl.when(fast)
    def _():
        _mix_chunks(True, *mix_args)

    @pl.when(jnp.logical_not(fast))
    def _():
        _mix_chunks(False, *mix_args)

    @pl.when(t == pl.num_programs(1) - 1)
    def _():
        for h in range(HEADS):
            sh_ref[0, h] = sht[h].T


def _const_spec(shape):
    nd = len(shape)
    return pl.BlockSpec(shape, lambda *_: (0,) * nd, pipeline_mode=pl.Buffered(1))


def _mix_prompt(x, w_in_b, lb_logits, gnh, gnr):
    bsz, seq, _ = x.shape
    nt = seq // PROMPT_TILE
    cos2, sin2 = _rope_tables(jnp.arange(seq, dtype=F32))
    dmat, cross, upd, cdec = _ret_consts(CHUNK)
    lane = lambda a: jnp.repeat(a, HEAD_DIM, axis=-1)
    cross_l = lane(cross.T)
    upd_l = lane(upd.T)
    cdec_l = lane(cdec[None, :])
    tril = np.tril(np.ones((CHUNK, CHUNK), np.float32))
    tril3 = jnp.asarray(np.concatenate([tril] * 3, axis=1), BF16)
    masks = jnp.asarray(_level_masks())
    act = pltpu.VMEM((PROMPT_TILE, WIDTH), F32)
    state_shape = jax.ShapeDtypeStruct((bsz, HEADS, HEAD_DIM, HEAD_DIM), F32)
    state_spec = pl.BlockSpec((1, HEADS, HEAD_DIM, HEAD_DIM), lambda b, t: (b, 0, 0, 0))
    rope_spec = pl.BlockSpec((PROMPT_TILE, HEAD_DIM), lambda b, t: (t, 0))
    return pl.pallas_call(
        _mix_prompt_kernel,
        grid=(bsz, nt),
        in_specs=[
            pl.BlockSpec((1, PROMPT_TILE, D_MODEL), lambda b, t: (b, t, 0)),
            _const_spec((D_MODEL, IN_COLS)),
            _const_spec((2, WIDTH)),
            _const_spec((1, WIDTH)),
            _const_spec((1, WIDTH)),
            rope_spec,
            rope_spec,
            _const_spec((CHUNK, 3 * CHUNK)),
            _const_spec((len(LEVELS) + 2, CHUNK, CHUNK)),
            _const_spec((HEADS, CHUNK, CHUNK)),
            _const_spec((CHUNK, WIDTH)),
            _const_spec((CHUNK, WIDTH)),
            _const_spec((1, WIDTH)),
        ],
        out_specs=[
            pl.BlockSpec((1, PROMPT_TILE, 2 * WIDTH), lambda b, t: (b, t, 0)),
            state_spec,
            state_spec,
        ],
        out_shape=[jax.ShapeDtypeStruct((bsz, seq, 2 * WIDTH), BF16), state_shape, state_shape],
        scratch_shapes=[act] * 10 + [pltpu.VMEM((HEADS, HEAD_DIM, HEAD_DIM), F32)],
        compiler_params=pltpu.CompilerParams(
            dimension_semantics=("arbitrary", "arbitrary"), vmem_limit_bytes=VMEM_LIMIT),
        name="mix_prompt",
    )(x, w_in_b, lb_logits, gnh, gnr, cos2, sin2, tril3, masks, dmat, cross_l, upd_l, cdec_l)


def _inproj_sample_kernel(x_ref, w_ref, lbl_ref, cos_ref, sin_ref,
                          qh, kh, vh, gh, gth, qr, kr, vr, gtr):
    lb = _lower_bound(lbl_ref)
    _project_activate(x_ref[...].astype(BF16), w_ref, lb, cos_ref[...], sin_ref[...],
                      qh, kh, vh, gh, gth, qr, kr, vr, gtr)


def _inproj_sample(x2d, w_in_b, lb_logits, cos2, sin2):
    n = x2d.shape[0]
    return pl.pallas_call(
        _inproj_sample_kernel,
        out_shape=(jax.ShapeDtypeStruct((n, WIDTH), F32),) * 9,
        compiler_params=pltpu.CompilerParams(vmem_limit_bytes=VMEM_LIMIT),
        name="inproj_sample",
    )(x2d, w_in_b, lb_logits, cos2, sin2)


def _mix_sample_kernel(rc_ref, qh, kh, vh, gh, gth, qr, kr, vr, gtr, shin, srin, gnh_ref, gnr_ref,
                       mix_ref, shout, srout, qd_s, kd_s, v_s, o_s):
    nseq = SAMPLE_SEQS
    nt = qh.shape[0]
    pad_rows = jnp.zeros((HEAD_DIM - nseq, HEAD_DIM), F32)

    def per_sequence(h, s_in, s_out, decay):
        for bb in range(nseq):
            seq_rows = pl.ds(bb, nt, stride=nseq)
            s = s_in[bb, h]
            o_s[seq_rows, :] = _dot(qd_s[seq_rows, :].astype(BF16), s.astype(BF16))
            s_out[bb, h] = decay(s, bb) + _dot_tn(kd_s[seq_rows, :].astype(BF16),
                                                  v_s[seq_rows, :].astype(BF16))

    for h in range(HEADS):
        hs = slice(h * HEAD_DIM, (h + 1) * HEAD_DIM)
        q = [qh[t, :, hs] for t in range(nt)]
        k = [kh[t, :, hs] for t in range(nt)]
        v = [vh[t, :, hs] for t in range(nt)]
        b = []
        for t in range(nt):
            g = gh[t, :, hs]
            b.append(g if t == 0 else b[-1] + g)
        b_last = b[-1]
        o = []
        for i in range(nt):
            acc = None
            for j in range(i + 1):
                w = q[i] * k[j]
                if j < i:
                    w = w * jnp.exp2(b[i] - b[j])
                term = jnp.sum(w, axis=-1, keepdims=True) * v[j]
                acc = term if acc is None else acc + term
            o.append(acc)
        for t in range(nt):
            tr = slice(t * nseq, (t + 1) * nseq)
            qd_s[tr, :] = q[t] * jnp.exp2(b[t])
            kd_s[tr, :] = k[t] * jnp.exp2(b_last - b[t])
            v_s[tr, :] = v[t]
        dcol = jnp.concatenate([jnp.exp2(b_last), pad_rows], axis=0).T
        per_sequence(h, shin, shout, lambda s, bb: s * dcol[:, bb:bb + 1])
        for t in range(nt):
            tr = slice(t * nseq, (t + 1) * nseq)
            mix_ref[t, :, hs] = _head_norm_gate(o[t] + o_s[tr, :], gnh_ref[:, hs],
                                                gth[t, :, hs]).astype(BF16)
        q = [qr[t, :, hs] for t in range(nt)]
        k = [kr[t, :, hs] for t in range(nt)]
        v = [vr[t, :, hs] for t in range(nt)]
        o = []
        for i in range(nt):
            acc = None
            for j in range(i + 1):
                a_ij = jnp.sum(q[i] * k[j], axis=-1, keepdims=True) * rc_ref[h, i * nt + j]
                term = a_ij * v[j]
                acc = term if acc is None else acc + term
            o.append(acc)
        for t in range(nt):
            tr = slice(t * nseq, (t + 1) * nseq)
            qd_s[tr, :] = q[t]
            kd_s[tr, :] = k[t] * rc_ref[h, nt * nt + nt + t]
            v_s[tr, :] = v[t]
        cdec = rc_ref[h, nt * nt + 2 * nt]
        per_sequence(h, srin, srout, lambda s, bb: s * cdec)
        rs = slice(WIDTH + h * HEAD_DIM, WIDTH + (h + 1) * HEAD_DIM)
        for t in range(nt):
            tr = slice(t * nseq, (t + 1) * nseq)
            cross_t = rc_ref[h, nt * nt + t]
            mix_ref[t, :, rs] = _head_norm_gate(o[t] + o_s[tr, :] * cross_t, gnr_ref[:, hs],
                                                gtr[t, :, hs]).astype(BF16)


def _mix_sample(acts, state_h, state_r, gnh, gnr, nt, nb):
    dmat, cross, upd, cdec = _ret_consts(nt)
    rc = jnp.concatenate([dmat.reshape(HEADS, nt * nt), cross, upd, cdec[:, None]], axis=1)
    acts3 = [a.reshape(nt, nb, WIDTH) for a in acts]
    act_spec = pl.BlockSpec((nt, SAMPLE_SEQS, WIDTH), lambda i: (0, i, 0))
    st_spec = pl.BlockSpec((SAMPLE_SEQS, HEADS, HEAD_DIM, HEAD_DIM), lambda i: (i, 0, 0, 0))
    st_shape = jax.ShapeDtypeStruct((nb, HEADS, HEAD_DIM, HEAD_DIM), F32)
    row_s = pltpu.VMEM((nt * SAMPLE_SEQS, HEAD_DIM), F32)
    return pl.pallas_call(
        _mix_sample_kernel,
        grid=(nb // SAMPLE_SEQS,),
        in_specs=[pl.BlockSpec(memory_space=pltpu.SMEM)] + [act_spec] * 9 + [st_spec, st_spec]
                 + [pl.BlockSpec((1, WIDTH), lambda i: (0, 0))] * 2,
        out_specs=[pl.BlockSpec((nt, SAMPLE_SEQS, 2 * WIDTH), lambda i: (0, i, 0)), st_spec, st_spec],
        out_shape=[jax.ShapeDtypeStruct((nt, nb, 2 * WIDTH), BF16), st_shape, st_shape],
        scratch_shapes=[row_s] * 4,
        compiler_params=pltpu.CompilerParams(
            dimension_semantics=("arbitrary",), vmem_limit_bytes=VMEM_LIMIT),
        name="mix_sample",
    )(rc, *acts3, state_h, state_r, gnh, gnr)


def _tail_kernel(mix_ref, x_ref, wo_ref, g1_ref, b1_ref, wg_ref, wu_ref, wd_ref, g2_ref, b2_ref,
                 y_ref):
    m = _dot(mix_ref[...], wo_ref[...])
    x1 = _layer_norm(DEEPNORM_ALPHA * x_ref[...] + m, g1_ref[...], b1_ref[...])
    xb = x1.astype(BF16)
    hidden = (_silu(_dot(xb, wg_ref[...])) * _dot(xb, wu_ref[...])).astype(BF16)
    d = _dot(hidden, wd_ref[...])
    y_ref[...] = _layer_norm(DEEPNORM_ALPHA * x1 + d, g2_ref[...], b2_ref[...])


def _tail(mix2d, x2d, wo, g1, b1, wg, wu, wd, g2, b2):
    n = x2d.shape[0]
    row_spec = lambda: pl.BlockSpec((TAIL_TILE, D_MODEL), lambda i: (i, 0))
    return pl.pallas_call(
        _tail_kernel,
        grid=(n // TAIL_TILE,),
        in_specs=[row_spec(), row_spec(),
                  _const_spec((D_MODEL, D_MODEL)), _const_spec((1, D_MODEL)), _const_spec((1, D_MODEL)),
                  _const_spec((D_MODEL, D_FF)), _const_spec((D_MODEL, D_FF)), _const_spec((D_FF, D_MODEL)),
                  _const_spec((1, D_MODEL)), _const_spec((1, D_MODEL))],
        out_specs=row_spec(),
        out_shape=jax.ShapeDtypeStruct((n, D_MODEL), F32),
        compiler_params=pltpu.CompilerParams(
            dimension_semantics=("arbitrary",), vmem_limit_bytes=VMEM_LIMIT),
        name="tail",
    )(mix2d, x2d, wo, g1, b1, wg, wu, wd, g2, b2)


def kernel(x_prompt, x_sample, state_hgrn, state_ret, w_in, lb_logits, hgrn_norm_g, ret_norm_g,
           w_out, ln1_g, ln1_b, w_gate, w_up, w_down, ln2_g, ln2_b):
    bsz, seq, _ = x_prompt.shape
    nb, nt, _ = x_sample.shape
    w_in_b = w_in[0].astype(BF16)
    tail_w = (w_out[0].astype(BF16), ln1_g, ln1_b, w_gate[0].astype(BF16), w_up[0].astype(BF16),
              w_down[0].astype(BF16), ln2_g, ln2_b)
    gnh = hgrn_norm_g[0].reshape(1, WIDTH)
    gnr = ret_norm_g[0].reshape(1, WIDTH)

    mix_p, sh_p, sr_p = _mix_prompt(x_prompt, w_in_b, lb_logits, gnh, gnr)
    y_p = _tail(mix_p.reshape(bsz * seq, 2 * WIDTH), x_prompt.reshape(bsz * seq, D_MODEL), *tail_w)

    xs = jnp.transpose(x_sample, (1, 0, 2)).reshape(nt * nb, D_MODEL)
    pos = jnp.repeat(jnp.arange(nt, dtype=F32) + float(PAST_LEN), nb)
    cos2, sin2 = _rope_tables(pos)
    acts = _inproj_sample(xs, w_in_b, lb_logits, cos2, sin2)
    mix_s, sh_s, sr_s = _mix_sample(acts, state_hgrn[0], state_ret[0], gnh, gnr, nt, nb)
    y_s = _tail(mix_s.reshape(nt * nb, 2 * WIDTH), xs, *tail_w)
    y_s = jnp.transpose(y_s.reshape(nt, nb, D_MODEL), (1, 0, 2))

    return (y_p.reshape(bsz, seq, D_MODEL), y_s, sh_p[None], sr_p[None], sh_s[None], sr_s[None])
```

```python
import functools
import math

import numpy as np
import jax
import jax.numpy as jnp
from jax import lax
from jax.experimental import pallas as pl
from jax.experimental.pallas import tpu as pltpu

F32 = jnp.float32
BF16 = jnp.bfloat16

D_MODEL = 1024
HEADS = 4
HEAD_DIM = 128
WIDTH = HEADS * HEAD_DIM
IN_COLS = 8 * WIDTH
D_FF = 2816
CHUNK = 128
ROPE_BASE = 10000.0
PAST_LEN = 16384
DEEPNORM_ALPHA = 2.0 ** 0.25
EPS = 1e-5
LOG2_E = 1.4426950408889634

LANES = 128
SUBLANES = 8
PROMPT_TILE = 512
TAIL_TILE = 512
SAMPLE_SEQS = 8
VMEM_LIMIT = 56 * 1024 * 1024

LEVELS = (64, 32, 16, 8, 4, 2, 1)
FAST_BLOCK = 64
FAST_SPAN_BITS = 100.0
FAST_Q_MAX = 2.0 ** 20


def _sigmoid_pair(x):
    e = jnp.exp(-jnp.abs(x))
    r = 1.0 / (1.0 + e)
    er = e * r
    pos = x >= 0
    return jnp.where(pos, r, er), jnp.where(pos, er, r)


def _silu(x):
    s, _ = _sigmoid_pair(x)
    return x * s


def _dot(a, b):
    return jnp.dot(a, b, preferred_element_type=F32)


def _dot_nt(a, b):
    return lax.dot_general(a, b, (((1,), (1,)), ((), ())), preferred_element_type=F32)


def _dot_tn(a, b):
    return lax.dot_general(a, b, (((0,), (0,)), ((), ())), preferred_element_type=F32)


def _split3(x):
    hi = x.astype(BF16)
    r = x - hi.astype(F32)
    mid = r.astype(BF16)
    lo = (r - mid.astype(F32)).astype(BF16)
    return hi, mid, lo


def _lower_bound(lbl_ref):
    l = lbl_ref[...]
    l0 = l[0:1, :]
    l1 = l[1:2, :]
    m = jnp.maximum(l0, l1)
    e0 = jnp.exp(l0 - m)
    e1 = jnp.exp(l1 - m)
    return e0 / (e0 + e1)


def _head_norm_gate(o, gain, gate):
    ms = jnp.mean(o * o, axis=-1, keepdims=True)
    return o * lax.rsqrt(ms + EPS) * gain * gate


def _layer_norm(h, g, b):
    mu = jnp.mean(h, axis=-1, keepdims=True)
    d = h - mu
    var = jnp.mean(d * d, axis=-1, keepdims=True)
    return d * lax.rsqrt(var + EPS) * g + b


def _project_activate(xb, w_ref, lb, cos2, sin2, qh, kh, vh, gh, gth, qr, kr, vr, gtr):
    def proj(g):
        return _dot(xb, w_ref[:, g * WIDTH:(g + 1) * WIDTH])

    qh[...] = _silu(proj(0))
    s_pos, s_neg = _sigmoid_pair(proj(1))
    gh[...] = jnp.log(lb + (1.0 - lb) * s_pos) * LOG2_E
    kh[...] = (1.0 - lb) * s_neg
    vh[...] = proj(2)
    gth[...] = _silu(proj(3))

    def rotary(p, scale):
        for h in range(HEADS):
            hs = slice(h * HEAD_DIM, (h + 1) * HEAD_DIM)
            xh = p[:, hs]
            r = xh * cos2 + pltpu.roll(xh, HEAD_DIM // 2, 1) * sin2
            yield hs, (r if scale is None else r * scale)

    for hs, r in rotary(proj(4), None):
        qr[:, hs] = r
    for hs, r in rotary(proj(5), HEAD_DIM ** -0.5):
        kr[:, hs] = r
    vr[...] = proj(6)
    gtr[...] = _silu(proj(7))


def _rope_kernel(hi_ref, lo_ref, inv_ref, sign_ref, cos_ref, sin_ref):
    inv = inv_ref[...]
    a = hi_ref[...] * inv
    b = lo_ref[...] * inv
    ca, sa = jnp.cos(a), jnp.sin(a)
    cb, sb = jnp.cos(b), jnp.sin(b)
    n_lo = lo_ref.shape[0]
    for i in range(hi_ref.shape[0]):
        rows = slice(i * n_lo, (i + 1) * n_lo)
        cos_ref[rows, :] = ca[i:i + 1, :] * cb - sa[i:i + 1, :] * sb
        sin_ref[rows, :] = (sa[i:i + 1, :] * cb + ca[i:i + 1, :] * sb) * sign_ref[...]


def _rope_tables(pos_hi, pos_lo):
    n = pos_hi.shape[0] * pos_lo.shape[0]
    half = HEAD_DIM // 2
    inv = ROPE_BASE ** (-jnp.arange(half, dtype=F32) / half)
    inv2 = jnp.concatenate([inv, inv])[None, :]
    sign = jnp.asarray(np.concatenate([-np.ones(half), np.ones(half)])[None, :], F32)
    return pl.pallas_call(
        _rope_kernel,
        out_shape=(jax.ShapeDtypeStruct((n, HEAD_DIM), F32),) * 2,
        name="rope_tables",
    )(pos_hi[:, None], pos_lo[:, None], inv2, sign)


def _ret_consts(c):
    lg = jnp.log1p(-jnp.exp2(-5.0 - jnp.arange(HEADS, dtype=F32)))
    idx = jnp.arange(c, dtype=F32)
    mask = jnp.tril(jnp.ones((c, c), dtype=bool))
    rel = jnp.where(mask, idx[:, None] - idx[None, :], 0.0)
    dmat = jnp.where(mask[None], jnp.exp(lg[:, None, None] * rel[None]), 0.0)
    cross = jnp.exp(lg[:, None] * (idx + 1.0)[None, :])
    upd = jnp.exp(lg[:, None] * (c - 1.0 - idx)[None, :])
    cdec = jnp.exp(lg * c)
    return dmat, cross, upd, cdec


def _level_masks():
    i = np.arange(CHUNK)[:, None]
    j = np.arange(CHUNK)[None, :]
    ms = [(i == j)]
    for s in LEVELS:
        ms.append((i // (2 * s) == j // (2 * s)) & (i % (2 * s) >= s) & (j % (2 * s) < s))
    ms.append((i // FAST_BLOCK == j // FAST_BLOCK) & (i >= j))
    return np.stack(ms).astype(np.float32)


def _level_factor(s, b, q, k, sub):
    def row(r):
        return jnp.broadcast_to(b[r:r + 1, :], (SUBLANES, LANES))

    es, ws = [], []
    for m in range(CHUNK // SUBLANES):
        r0 = m * SUBLANES
        qb = q[r0:r0 + SUBLANES, :]
        kb = k[r0:r0 + SUBLANES, :]
        bb = b[r0:r0 + SUBLANES, :]
        if s >= SUBLANES:
            ref = row((r0 // (2 * s)) * 2 * s + s - 1)
            lower = (r0 % (2 * s)) >= s
            es.append(bb - ref if lower else ref - bb)
            ws.append(qb if lower else kb)
        elif s == 4:
            es.append(-jnp.abs(bb - row(r0 + 3)))
            ws.append(jnp.where(sub >= 4, qb, kb))
        elif s == 2:
            es.append(-jnp.abs(bb - jnp.where(sub < 4, row(r0 + 1), row(r0 + 5))))
            ws.append(jnp.where((sub & 3) >= 2, qb, kb))
        else:
            ref = jnp.where(sub < 2, row(r0),
                            jnp.where(sub < 4, row(r0 + 2), jnp.where(sub < 6, row(r0 + 4), row(r0 + 6))))
            es.append(-jnp.abs(bb - ref))
            ws.append(jnp.where((sub & 1) == 1, qb, kb))
    e = jnp.concatenate(es, axis=0)
    w = jnp.concatenate(ws, axis=0)
    return (w * jnp.exp2(e)).astype(BF16)


def _fast_block_factors(b, q, k):
    refs = []
    for blk in range(CHUNK // FAST_BLOCK):
        r = blk * FAST_BLOCK + FAST_BLOCK // 2 - 1
        refs.append(jnp.broadcast_to(b[r:r + 1, :], (FAST_BLOCK, LANES)))
    e = b - jnp.concatenate(refs, axis=0)
    return (q * jnp.exp2(e)).astype(BF16), (k * jnp.exp2(-e)).astype(BF16)


def _mix_chunks(fast, acts, b_s, sht, sr_ref, mix_ref, gnh_ref, gnr_ref, masks_ref, dmat_ref,
                cross_ref, upd_ref, cdec_ref):
    qh, kh, vh, _, gth, qr, kr, vr, gtr = acts
    sub = lax.broadcasted_iota(jnp.int32, (SUBLANES, LANES), 0)
    for ci in range(PROMPT_TILE // CHUNK):
        rows = slice(ci * CHUNK, (ci + 1) * CHUNK)
        for h in range(HEADS):
            hs = slice(h * HEAD_DIM, (h + 1) * HEAD_DIM)
            q = qh[rows, hs]
            k = kh[rows, hs]
            v = vh[rows, hs].astype(BF16)
            b = b_s[rows, hs]
            b_last = b[CHUNK - 1:CHUNK, :]
            if fast:
                qf, kf = _fast_block_factors(b, q, k)
                a = jnp.where(masks_ref[len(LEVELS) + 1] > 0.5, _dot_nt(qf, kf), 0.0)
            else:
                a = _dot_nt(q.astype(BF16), k.astype(BF16)) * masks_ref[0]
            for li, s in enumerate(LEVELS):
                if fast and s < FAST_BLOCK:
                    continue
                xl = _level_factor(s, b, q, k, sub)
                a = a + _dot_nt(xl, xl) * masks_ref[li + 1]
            st = sht[h]
            o = _dot(a.astype(BF16), v) + _dot_nt((q * jnp.exp2(b)).astype(BF16), st.astype(BF16))
            kd = (k * jnp.exp2(b_last - b)).astype(BF16)
            sht[h] = st * jnp.exp2(b_last) + _dot_tn(v, kd)
            mix_ref[0, rows, hs] = _head_norm_gate(o, gnh_ref[:, hs], gth[rows, hs]).astype(BF16)
            q = qr[rows, hs]
            k = kr[rows, hs]
            v = vr[rows, hs].astype(BF16)
            a = _dot_nt(q.astype(BF16), k.astype(BF16)) * dmat_ref[h]
            s_r = sr_ref[0, h]
            qa = jnp.concatenate([(q * cross_ref[:, hs]).astype(BF16), a.astype(BF16)], axis=1)
            o = _dot(qa, jnp.concatenate([s_r.astype(BF16), v], axis=0))
            ku = (k * upd_ref[:, hs]).astype(BF16)
            sr_ref[0, h] = s_r * cdec_ref[:, hs] + _dot_tn(ku, v)
            rs = slice(WIDTH + h * HEAD_DIM, WIDTH + (h + 1) * HEAD_DIM)
            mix_ref[0, rows, rs] = _head_norm_gate(o, gnr_ref[:, hs], gtr[rows, hs]).astype(BF16)


def _mix_prompt_kernel(x_ref, w_ref, lbl_ref, gnh_ref, gnr_ref, cos_ref, sin_ref,
                       tril_ref, masks_ref, dmat_ref, cross_ref, upd_ref, cdec_ref,
                       mix_ref, sh_ref, sr_ref, *scratch):
    acts, b_s, sht = scratch[:9], scratch[9], scratch[10]
    t = pl.program_id(1)

    @pl.when(t == 0)
    def _():
        sht[...] = jnp.zeros_like(sht)
        sr_ref[...] = jnp.zeros_like(sr_ref)

    lb = _lower_bound(lbl_ref)
    _project_activate(x_ref[0].astype(BF16), w_ref, lb, cos_ref[...], sin_ref[...], *acts)

    qh, gh = acts[0], acts[3]
    span = None
    for ci in range(PROMPT_TILE // CHUNK):
        rows = slice(ci * CHUNK, (ci + 1) * CHUNK)
        b_all = _dot(tril_ref[...], jnp.concatenate(_split3(gh[rows, :]), axis=0))
        b_s[rows, :] = b_all
        prev = None
        for r in range(FAST_BLOCK // 2 - 1, CHUNK, FAST_BLOCK // 2):
            cur = b_all[r:r + 1, :]
            d = -cur if prev is None else prev - cur
            span = d if span is None else jnp.maximum(span, d)
            prev = cur
    fast = (jnp.max(span) < FAST_SPAN_BITS) & (jnp.max(jnp.abs(qh[...])) < FAST_Q_MAX)
    mix_args = (acts, b_s, sht, sr_ref, mix_ref, gnh_ref, gnr_ref, masks_ref, dmat_ref, cross_ref,
                upd_ref, cdec_ref)

    @pl.when(fast)
    def _():
        _mix_chunks(True, *mix_args)

    @pl.when(jnp.logical_not(fast))
    def _():
        _mix_chunks(False, *mix_args)

    @pl.when(t == pl.num_programs(1) - 1)
    def _():
        for h in range(HEADS):
            sh_ref[0, h] = sht[h].T


def _const_spec(shape):
    nd = len(shape)
    return pl.BlockSpec(shape, lambda *_: (0,) * nd, pipeline_mode=pl.Buffered(1))


def _mix_prompt(x, w_in_b, lb_logits, gnh, gnr):
    bsz, seq, _ = x.shape
    nt = seq // PROMPT_TILE
    cos2, sin2 = _rope_tables(jnp.arange(0, seq, LANES, dtype=F32), jnp.arange(LANES, dtype=F32))
    dmat, cross, upd, cdec = _ret_consts(CHUNK)
    lane = lambda a: jnp.repeat(a, HEAD_DIM, axis=-1)
    cross_l = lane(cross.T)
    upd_l = lane(upd.T)
    cdec_l = lane(cdec[None, :])
    tril = np.tril(np.ones((CHUNK, CHUNK), np.float32))
    tril3 = jnp.asarray(np.concatenate([tril] * 3, axis=1), BF16)
    masks = jnp.asarray(_level_masks())
    act = pltpu.VMEM((PROMPT_TILE, WIDTH), F32)
    state_shape = jax.ShapeDtypeStruct((bsz, HEADS, HEAD_DIM, HEAD_DIM), F32)
    state_spec = pl.BlockSpec((1, HEADS, HEAD_DIM, HEAD_DIM), lambda b, t: (b, 0, 0, 0))
    rope_spec = pl.BlockSpec((PROMPT_TILE, HEAD_DIM), lambda b, t: (t, 0))
    return pl.pallas_call(
        _mix_prompt_kernel,
        grid=(bsz, nt),
        in_specs=[
            pl.BlockSpec((1, PROMPT_TILE, D_MODEL), lambda b, t: (b, t, 0)),
            _const_spec((D_MODEL, IN_COLS)),
            _const_spec((2, WIDTH)),
            _const_spec((1, WIDTH)),
            _const_spec((1, WIDTH)),
            rope_spec,
            rope_spec,
            _const_spec((CHUNK, 3 * CHUNK)),
            _const_spec((len(LEVELS) + 2, CHUNK, CHUNK)),
            _const_spec((HEADS, CHUNK, CHUNK)),
            _const_spec((CHUNK, WIDTH)),
            _const_spec((CHUNK, WIDTH)),
            _const_spec((1, WIDTH)),
        ],
        out_specs=[
            pl.BlockSpec((1, PROMPT_TILE, 2 * WIDTH), lambda b, t: (b, t, 0)),
            state_spec,
            state_spec,
        ],
        out_shape=[jax.ShapeDtypeStruct((bsz, seq, 2 * WIDTH), BF16), state_shape, state_shape],
        scratch_shapes=[act] * 10 + [pltpu.VMEM((HEADS, HEAD_DIM, HEAD_DIM), F32)],
        compiler_params=pltpu.CompilerParams(
            dimension_semantics=("arbitrary", "arbitrary"), vmem_limit_bytes=VMEM_LIMIT),
        name="mix_prompt",
    )(x, w_in_b, lb_logits, gnh, gnr, cos2, sin2, tril3, masks, dmat, cross_l, upd_l, cdec_l)


def _inproj_sample_kernel(x_ref, w_ref, lbl_ref, cos_ref, sin_ref,
                          qh, kh, vh, gh, gth, qr, kr, vr, gtr):
    lb = _lower_bound(lbl_ref)
    _project_activate(x_ref[...].astype(BF16), w_ref, lb, cos_ref[...], sin_ref[...],
                      qh, kh, vh, gh, gth, qr, kr, vr, gtr)


def _inproj_sample(x2d, w_in_b, lb_logits, cos2, sin2):
    n = x2d.shape[0]
    return pl.pallas_call(
        _inproj_sample_kernel,
        out_shape=(jax.ShapeDtypeStruct((n, WIDTH), F32),) * 9,
        compiler_params=pltpu.CompilerParams(vmem_limit_bytes=VMEM_LIMIT),
        name="inproj_sample",
    )(x2d, w_in_b, lb_logits, cos2, sin2)


def _mix_sample_kernel(rc_ref, qh, kh, vh, gh, gth, qr, kr, vr, gtr, shin, srin, gnh_ref, gnr_ref,
                       mix_ref, shout, srout, qd_s, kd_s, v_s, o_s):
    nseq = SAMPLE_SEQS
    nt = qh.shape[0]
    pad_rows = jnp.zeros((HEAD_DIM - nseq, HEAD_DIM), F32)

    def per_sequence(h, s_in, s_out, decay):
        for bb in range(nseq):
            seq_rows = pl.ds(bb, nt, stride=nseq)
            s = s_in[bb, h]
            o_s[seq_rows, :] = _dot(qd_s[seq_rows, :].astype(BF16), s.astype(BF16))
            s_out[bb, h] = decay(s, bb) + _dot_tn(kd_s[seq_rows, :].astype(BF16),
                                                  v_s[seq_rows, :].astype(BF16))

    for h in range(HEADS):
        hs = slice(h * HEAD_DIM, (h + 1) * HEAD_DIM)
        q = [qh[t, :, hs] for t in range(nt)]
        k = [kh[t, :, hs] for t in range(nt)]
        v = [vh[t, :, hs] for t in range(nt)]
        b = []
        for t in range(nt):
            g = gh[t, :, hs]
            b.append(g if t == 0 else b[-1] + g)
        b_last = b[-1]
        o = []
        for i in range(nt):
            acc = None
            for j in range(i + 1):
                w = q[i] * k[j]
                if j < i:
                    w = w * jnp.exp2(b[i] - b[j])
                term = jnp.sum(w, axis=-1, keepdims=True) * v[j]
                acc = term if acc is None else acc + term
            o.append(acc)
        for t in range(nt):
            tr = slice(t * nseq, (t + 1) * nseq)
            qd_s[tr, :] = q[t] * jnp.exp2(b[t])
            kd_s[tr, :] = k[t] * jnp.exp2(b_last - b[t])
            v_s[tr, :] = v[t]
        dcol = jnp.concatenate([jnp.exp2(b_last), pad_rows], axis=0).T
        per_sequence(h, shin, shout, lambda s, bb: s * dcol[:, bb:bb + 1])
        for t in range(nt):
            tr = slice(t * nseq, (t + 1) * nseq)
            mix_ref[t, :, hs] = _head_norm_gate(o[t] + o_s[tr, :], gnh_ref[:, hs],
                                                gth[t, :, hs]).astype(BF16)
        q = [qr[t, :, hs] for t in range(nt)]
        k = [kr[t, :, hs] for t in range(nt)]
        v = [vr[t, :, hs] for t in range(nt)]
        o = []
        for i in range(nt):
            acc = None
            for j in range(i + 1):
                a_ij = jnp.sum(q[i] * k[j], axis=-1, keepdims=True) * rc_ref[h, i * nt + j]
                term = a_ij * v[j]
                acc = term if acc is None else acc + term
            o.append(acc)
        for t in range(nt):
            tr = slice(t * nseq, (t + 1) * nseq)
            qd_s[tr, :] = q[t]
            kd_s[tr, :] = k[t] * rc_ref[h, nt * nt + nt + t]
            v_s[tr, :] = v[t]
        cdec = rc_ref[h, nt * nt + 2 * nt]
        per_sequence(h, srin, srout, lambda s, bb: s * cdec)
        rs = slice(WIDTH + h * HEAD_DIM, WIDTH + (h + 1) * HEAD_DIM)
        for t in range(nt):
            tr = slice(t * nseq, (t + 1) * nseq)
            cross_t = rc_ref[h, nt * nt + t]
            mix_ref[t, :, rs] = _head_norm_gate(o[t] + o_s[tr, :] * cross_t, gnr_ref[:, hs],
                                                gtr[t, :, hs]).astype(BF16)


def _mix_sample(acts, state_h, state_r, gnh, gnr, nt, nb):
    dmat, cross, upd, cdec = _ret_consts(nt)
    rc = jnp.concatenate([dmat.reshape(HEADS, nt * nt), cross, upd, cdec[:, None]], axis=1)
    acts3 = [a.reshape(nt, nb, WIDTH) for a in acts]
    act_spec = pl.BlockSpec((nt, SAMPLE_SEQS, WIDTH), lambda i: (0, i, 0))
    st_spec = pl.BlockSpec((SAMPLE_SEQS, HEADS, HEAD_DIM, HEAD_DIM), lambda i: (i, 0, 0, 0))
    st_shape = jax.ShapeDtypeStruct((nb, HEADS, HEAD_DIM, HEAD_DIM), F32)
    row_s = pltpu.VMEM((nt * SAMPLE_SEQS, HEAD_DIM), F32)
    return pl.pallas_call(
        _mix_sample_kernel,
        grid=(nb // SAMPLE_SEQS,),
        in_specs=[pl.BlockSpec(memory_space=pltpu.SMEM)] + [act_spec] * 9 + [st_spec, st_spec]
                 + [pl.BlockSpec((1, WIDTH), lambda i: (0, 0))] * 2,
        out_specs=[pl.BlockSpec((nt, SAMPLE_SEQS, 2 * WIDTH), lambda i: (0, i, 0)), st_spec, st_spec],
        out_shape=[jax.ShapeDtypeStruct((nt, nb, 2 * WIDTH), BF16), st_shape, st_shape],
        scratch_shapes=[row_s] * 4,
        compiler_params=pltpu.CompilerParams(
            dimension_semantics=("arbitrary",), vmem_limit_bytes=VMEM_LIMIT),
        name="mix_sample",
    )(rc, *acts3, state_h, state_r, gnh, gnr)


def _tail_rows(mix_ref, x_ref, y_ref, wo_ref, g1_ref, b1_ref, wg_ref, wu_ref, wd_ref, g2_ref, b2_ref):
    m = _dot(mix_ref[...], wo_ref[...])
    x1 = _layer_norm(DEEPNORM_ALPHA * x_ref[...] + m, g1_ref[...], b1_ref[...])
    xb = x1.astype(BF16)
    hidden = (_silu(_dot(xb, wg_ref[...])) * _dot(xb, wu_ref[...])).astype(BF16)
    d = _dot(hidden, wd_ref[...])
    y_ref[...] = _layer_norm(DEEPNORM_ALPHA * x1 + d, g2_ref[...], b2_ref[...])


def _tail_kernel(n_a, mix_a, x_a, mix_b, x_b, *rest):
    weights, (y_a, y_b) = rest[:-2], rest[-2:]
    i = pl.program_id(0)

    @pl.when(i < n_a)
    def _():
        _tail_rows(mix_a, x_a, y_a, *weights)

    @pl.when(i >= n_a)
    def _():
        _tail_rows(mix_b, x_b, y_b, *weights)


def _tail(mix_a, x_a, mix_b, x_b, wo, g1, b1, wg, wu, wd, g2, b2):
    n_a = x_a.shape[0] // TAIL_TILE
    n_b = x_b.shape[0] // TAIL_TILE
    spec_a = lambda: pl.BlockSpec((TAIL_TILE, D_MODEL), lambda i: (jnp.minimum(i, n_a - 1), 0))
    spec_b = lambda: pl.BlockSpec((TAIL_TILE, D_MODEL), lambda i: (jnp.maximum(i - n_a, 0), 0))
    return pl.pallas_call(
        functools.partial(_tail_kernel, n_a),
        grid=(n_a + n_b,),
        in_specs=[spec_a(), spec_a(), spec_b(), spec_b(),
                  _const_spec((D_MODEL, D_MODEL)), _const_spec((1, D_MODEL)), _const_spec((1, D_MODEL)),
                  _const_spec((D_MODEL, D_FF)), _const_spec((D_MODEL, D_FF)), _const_spec((D_FF, D_MODEL)),
                  _const_spec((1, D_MODEL)), _const_spec((1, D_MODEL))],
        out_specs=[spec_a(), spec_b()],
        out_shape=[jax.ShapeDtypeStruct(x_a.shape, F32), jax.ShapeDtypeStruct(x_b.shape, F32)],
        compiler_params=pltpu.CompilerParams(
            dimension_semantics=("arbitrary",), vmem_limit_bytes=VMEM_LIMIT),
        name="tail",
    )(mix_a, x_a, mix_b, x_b, wo, g1, b1, wg, wu, wd, g2, b2)


def kernel(x_prompt, x_sample, state_hgrn, state_ret, w_in, lb_logits, hgrn_norm_g, ret_norm_g,
           w_out, ln1_g, ln1_b, w_gate, w_up, w_down, ln2_g, ln2_b):
    bsz, seq, _ = x_prompt.shape
    nb, nt, _ = x_sample.shape
    w_in_b = w_in[0].astype(BF16)
    tail_w = (w_out[0].astype(BF16), ln1_g, ln1_b, w_gate[0].astype(BF16), w_up[0].astype(BF16),
              w_down[0].astype(BF16), ln2_g, ln2_b)
    gnh = hgrn_norm_g[0].reshape(1, WIDTH)
    gnr = ret_norm_g[0].reshape(1, WIDTH)

    mix_p, sh_p, sr_p = _mix_prompt(x_prompt, w_in_b, lb_logits, gnh, gnr)

    xs = jnp.transpose(x_sample, (1, 0, 2)).reshape(nt * nb, D_MODEL)
    cos2, sin2 = _rope_tables(jnp.arange(nt, dtype=F32) + float(PAST_LEN), jnp.zeros((nb,), F32))
    acts = _inproj_sample(xs, w_in_b, lb_logits, cos2, sin2)
    mix_s, sh_s, sr_s = _mix_sample(acts, state_hgrn[0], state_ret[0], gnh, gnr, nt, nb)
    y_p, y_s = _tail(mix_p.reshape(bsz * seq, 2 * WIDTH), x_prompt.reshape(bsz * seq, D_MODEL),
                     mix_s.reshape(nt * nb, 2 * WIDTH), xs, *tail_w)
    y_s = jnp.transpose(y_s.reshape(nt, nb, D_MODEL), (1, 0, 2))

    return (y_p.reshape(bsz, seq, D_MODEL), y_s, sh_p[None], sr_p[None], sh_s[None], sr_s[None])
```

```python
import numpy as np
import jax
import jax.numpy as jnp
from jax import lax
from jax.experimental import pallas as pl
from jax.experimental.pallas import tpu as pltpu

F32 = jnp.float32
BF16 = jnp.bfloat16

D_MODEL = 1024
HEADS = 4
HEAD_DIM = 128
WIDTH = HEADS * HEAD_DIM
IN_COLS = 8 * WIDTH
D_FF = 2816
CHUNK = 128
ROPE_BASE = 10000.0
PAST_LEN = 16384
DEEPNORM_ALPHA = 2.0 ** 0.25
EPS = 1e-5
LOG2_E = 1.4426950408889634

LANES = 128
SUBLANES = 8
PROMPT_TILE = 1024
TAIL_TILE = 1024
FF_TILE = 256
SAMPLE_SEQS = 16
VMEM_LIMIT = 60 * 1024 * 1024

LEVELS = (64, 32, 16, 8, 4, 2, 1)
FAST_BLOCK = 64
FAST_SPAN_BITS = 100.0
FAST_Q_MAX = 2.0 ** 20


def _sigmoid_pair(x):
    e = jnp.exp(-jnp.abs(x))
    r = 1.0 / (1.0 + e)
    er = e * r
    pos = x >= 0
    return jnp.where(pos, r, er), jnp.where(pos, er, r)


def _silu(x):
    return x / (1.0 + jnp.exp2(x * -LOG2_E))


def _dot(a, b):
    return jnp.dot(a, b, preferred_element_type=F32)


def _dot_nt(a, b):
    return lax.dot_general(a, b, (((1,), (1,)), ((), ())), preferred_element_type=F32)


def _dot_tn(a, b):
    return lax.dot_general(a, b, (((0,), (0,)), ((), ())), preferred_element_type=F32)


def _split3(x):
    hi = x.astype(BF16)
    r = x - hi.astype(F32)
    mid = r.astype(BF16)
    lo = (r - mid.astype(F32)).astype(BF16)
    return hi, mid, lo


def _lower_bound(lbl_ref):
    l = lbl_ref[...]
    l0 = l[0:1, :]
    l1 = l[1:2, :]
    m = jnp.maximum(l0, l1)
    e0 = jnp.exp(l0 - m)
    e1 = jnp.exp(l1 - m)
    return e0 / (e0 + e1)


def _head_norm_gate(o, gain, gate):
    ms = jnp.mean(o * o, axis=-1, keepdims=True)
    return o * lax.rsqrt(ms + EPS) * gain * gate


def _layer_norm(h, g, b):
    mu = jnp.mean(h, axis=-1, keepdims=True)
    d = h - mu
    var = jnp.mean(d * d, axis=-1, keepdims=True)
    return d * lax.rsqrt(var + EPS) * g + b


def _project_activate(xb, w_ref, lb, cos2, sin2, qh, kh, vh, gh, gth, qr, kr, vr, gtr):
    def proj(g):
        return _dot(xb, w_ref[:, g * WIDTH:(g + 1) * WIDTH])

    qh[...] = _silu(proj(0))
    s_pos, s_neg = _sigmoid_pair(proj(1))
    gh[...] = jnp.log(lb + (1.0 - lb) * s_pos) * LOG2_E
    kh[...] = (1.0 - lb) * s_neg
    vh[...] = proj(2)
    gth[...] = _silu(proj(3))

    def rotary(p, scale):
        for h in range(HEADS):
            hs = slice(h * HEAD_DIM, (h + 1) * HEAD_DIM)
            xh = p[:, hs]
            r = xh * cos2 + pltpu.roll(xh, HEAD_DIM // 2, 1) * sin2
            yield hs, (r if scale is None else r * scale)

    for hs, r in rotary(proj(4), None):
        qr[:, hs] = r
    for hs, r in rotary(proj(5), HEAD_DIM ** -0.5):
        kr[:, hs] = r
    vr[...] = proj(6)
    gtr[...] = _silu(proj(7))


def _rope_kernel(hi_ref, lo_ref, inv_ref, sign_ref, cos_ref, sin_ref):
    inv = inv_ref[...]
    a = hi_ref[...] * inv
    b = lo_ref[...] * inv
    ca, sa = jnp.cos(a), jnp.sin(a)
    cb, sb = jnp.cos(b), jnp.sin(b)
    n_lo = lo_ref.shape[0]
    for i in range(hi_ref.shape[0]):
        rows = slice(i * n_lo, (i + 1) * n_lo)
        cos_ref[rows, :] = ca[i:i + 1, :] * cb - sa[i:i + 1, :] * sb
        sin_ref[rows, :] = (sa[i:i + 1, :] * cb + ca[i:i + 1, :] * sb) * sign_ref[...]


def _rope_tables(pos_hi, pos_lo):
    n = pos_hi.shape[0] * pos_lo.shape[0]
    half = HEAD_DIM // 2
    inv = ROPE_BASE ** (-jnp.arange(half, dtype=F32) / half)
    inv2 = jnp.concatenate([inv, inv])[None, :]
    sign = jnp.asarray(np.concatenate([-np.ones(half), np.ones(half)])[None, :], F32)
    return pl.pallas_call(
        _rope_kernel,
        out_shape=(jax.ShapeDtypeStruct((n, HEAD_DIM), F32),) * 2,
        name="rope_tables",
    )(pos_hi[:, None], pos_lo[:, None], inv2, sign)


def _ret_consts(c):
    lg = jnp.log1p(-jnp.exp2(-5.0 - jnp.arange(HEADS, dtype=F32)))
    idx = jnp.arange(c, dtype=F32)
    mask = jnp.tril(jnp.ones((c, c), dtype=bool))
    rel = jnp.where(mask, idx[:, None] - idx[None, :], 0.0)
    dmat = jnp.where(mask[None], jnp.exp(lg[:, None, None] * rel[None]), 0.0)
    cross = jnp.exp(lg[:, None] * (idx + 1.0)[None, :])
    upd = jnp.exp(lg[:, None] * (c - 1.0 - idx)[None, :])
    cdec = jnp.exp(lg * c)
    return dmat, cross, upd, cdec


def _level_masks():
    i = np.arange(CHUNK)[:, None]
    j = np.arange(CHUNK)[None, :]
    ms = [(i == j)]
    for s in LEVELS:
        ms.append((i // (2 * s) == j // (2 * s)) & (i % (2 * s) >= s) & (j % (2 * s) < s))
    ms.append((i // FAST_BLOCK == j // FAST_BLOCK) & (i >= j))
    return np.stack(ms).astype(np.float32)


def _level_factor(s, b, q, k, sub):
    def row(r):
        return jnp.broadcast_to(b[r:r + 1, :], (SUBLANES, LANES))

    es, ws = [], []
    for m in range(CHUNK // SUBLANES):
        r0 = m * SUBLANES
        qb = q[r0:r0 + SUBLANES, :]
        kb = k[r0:r0 + SUBLANES, :]
        bb = b[r0:r0 + SUBLANES, :]
        if s >= SUBLANES:
            ref = row((r0 // (2 * s)) * 2 * s + s - 1)
            lower = (r0 % (2 * s)) >= s
            es.append(bb - ref if lower else ref - bb)
            ws.append(qb if lower else kb)
        elif s == 4:
            es.append(-jnp.abs(bb - row(r0 + 3)))
            ws.append(jnp.where(sub >= 4, qb, kb))
        elif s == 2:
            es.append(-jnp.abs(bb - jnp.where(sub < 4, row(r0 + 1), row(r0 + 5))))
            ws.append(jnp.where((sub & 3) >= 2, qb, kb))
        else:
            ref = jnp.where(sub < 2, row(r0),
                            jnp.where(sub < 4, row(r0 + 2), jnp.where(sub < 6, row(r0 + 4), row(r0 + 6))))
            es.append(-jnp.abs(bb - ref))
            ws.append(jnp.where((sub & 1) == 1, qb, kb))
    e = jnp.concatenate(es, axis=0)
    w = jnp.concatenate(ws, axis=0)
    return (w * jnp.exp2(e)).astype(BF16)


def _fast_block_factors(b, q, k):
    refs = []
    for blk in range(CHUNK // FAST_BLOCK):
        r = blk * FAST_BLOCK + FAST_BLOCK // 2 - 1
        refs.append(jnp.broadcast_to(b[r:r + 1, :], (FAST_BLOCK, LANES)))
    e = b - jnp.concatenate(refs, axis=0)
    return (q * jnp.exp2(e)).astype(BF16), (k * jnp.exp2(-e)).astype(BF16)


def _mix_chunks(fast, acts, b_s, sht, sr_ref, mix_ref, gnh_ref, gnr_ref, masks_ref, dmat_ref,
                cross_ref, upd_ref, cdec_ref):
    qh, kh, vh, _, gth, qr, kr, vr, gtr = acts
    sub = lax.broadcasted_iota(jnp.int32, (SUBLANES, LANES), 0)
    for ci in range(PROMPT_TILE // CHUNK):
        rows = slice(ci * CHUNK, (ci + 1) * CHUNK)
        for h in range(HEADS):
            hs = slice(h * HEAD_DIM, (h + 1) * HEAD_DIM)
            q = qh[rows, hs]
            k = kh[rows, hs]
            v = vh[rows, hs].astype(BF16)
            b = b_s[rows, hs]
            b_last = b[CHUNK - 1:CHUNK, :]
            if fast:
                qf, kf = _fast_block_factors(b, q, k)
                a = jnp.where(masks_ref[len(LEVELS) + 1] > 0.5, _dot_nt(qf, kf), 0.0)
            else:
                a = _dot_nt(q.astype(BF16), k.astype(BF16)) * masks_ref[0]
            for li, s in enumerate(LEVELS):
                if fast and s < FAST_BLOCK:
                    continue
                xl = _level_factor(s, b, q, k, sub)
                a = a + _dot_nt(xl, xl) * masks_ref[li + 1]
            st = sht[h]
            o = _dot(a.astype(BF16), v) + _dot_nt((q * jnp.exp2(b)).astype(BF16), st.astype(BF16))
            kd = (k * jnp.exp2(b_last - b)).astype(BF16)
            sht[h] = st * jnp.exp2(b_last) + _dot_tn(v, kd)
            mix_ref[0, rows, hs] = _head_norm_gate(o, gnh_ref[:, hs], gth[rows, hs]).astype(BF16)
            q = qr[rows, hs]
            k = kr[rows, hs]
            v = vr[rows, hs].astype(BF16)
            a = _dot_nt(q.astype(BF16), k.astype(BF16)) * dmat_ref[h]
            s_r = sr_ref[0, h]
            qa = jnp.concatenate([(q * cross_ref[:, hs]).astype(BF16), a.astype(BF16)], axis=1)
            o = _dot(qa, jnp.concatenate([s_r.astype(BF16), v], axis=0))
            ku = (k * upd_ref[:, hs]).astype(BF16)
            sr_ref[0, h] = s_r * cdec_ref[:, hs] + _dot_tn(ku, v)
            rs = slice(WIDTH + h * HEAD_DIM, WIDTH + (h + 1) * HEAD_DIM)
            mix_ref[0, rows, rs] = _head_norm_gate(o, gnr_ref[:, hs], gtr[rows, hs]).astype(BF16)


def _mix_prompt_kernel(x_ref, w_ref, lbl_ref, gnh_ref, gnr_ref, cos_ref, sin_ref,
                       tril_ref, masks_ref, dmat_ref, cross_ref, upd_ref, cdec_ref,
                       mix_ref, sh_ref, sr_ref, *scratch):
    acts, b_s, sht = scratch[:9], scratch[9], scratch[10]
    t = pl.program_id(1)

    @pl.when(t == 0)
    def _():
        sht[...] = jnp.zeros_like(sht)
        sr_ref[...] = jnp.zeros_like(sr_ref)

    lb = _lower_bound(lbl_ref)
    _project_activate(x_ref[0].astype(BF16), w_ref, lb, cos_ref[...], sin_ref[...], *acts)

    qh, gh = acts[0], acts[3]
    span = None
    for ci in range(PROMPT_TILE // CHUNK):
        rows = slice(ci * CHUNK, (ci + 1) * CHUNK)
        b_all = _dot(tril_ref[...], jnp.concatenate(_split3(gh[rows, :]), axis=0))
        b_s[rows, :] = b_all
        prev = None
        for r in range(FAST_BLOCK // 2 - 1, CHUNK, FAST_BLOCK // 2):
            cur = b_all[r:r + 1, :]
            d = -cur if prev is None else prev - cur
            span = d if span is None else jnp.maximum(span, d)
            prev = cur
    fast = (jnp.max(span) < FAST_SPAN_BITS) & (jnp.max(jnp.abs(qh[...])) < FAST_Q_MAX)
    mix_args = (acts, b_s, sht, sr_ref, mix_ref, gnh_ref, gnr_ref, masks_ref, dmat_ref, cross_ref,
                upd_ref, cdec_ref)

    @pl.when(fast)
    def _():
        _mix_chunks(True, *mix_args)

    @pl.when(jnp.logical_not(fast))
    def _():
        _mix_chunks(False, *mix_args)

    @pl.when(t == pl.num_programs(1) - 1)
    def _():
        for h in range(HEADS):
            sh_ref[0, h] = sht[h].T


def _const_spec(shape):
    nd = len(shape)
    return pl.BlockSpec(shape, lambda *_: (0,) * nd, pipeline_mode=pl.Buffered(1))


def _mix_prompt(x, w_in_b, lb_logits, gnh, gnr):
    bsz, seq, _ = x.shape
    nt = seq // PROMPT_TILE
    cos2, sin2 = _rope_tables(jnp.arange(0, seq, LANES, dtype=F32), jnp.arange(LANES, dtype=F32))
    dmat, cross, upd, cdec = _ret_consts(CHUNK)
    lane = lambda a: jnp.repeat(a, HEAD_DIM, axis=-1)
    cross_l = lane(cross.T)
    upd_l = lane(upd.T)
    cdec_l = lane(cdec[None, :])
    tril = np.tril(np.ones((CHUNK, CHUNK), np.float32))
    tril3 = jnp.asarray(np.concatenate([tril] * 3, axis=1), BF16)
    masks = jnp.asarray(_level_masks())
    act = pltpu.VMEM((PROMPT_TILE, WIDTH), F32)
    state_shape = jax.ShapeDtypeStruct((bsz, HEADS, HEAD_DIM, HEAD_DIM), F32)
    state_spec = pl.BlockSpec((1, HEADS, HEAD_DIM, HEAD_DIM), lambda b, t: (b, 0, 0, 0))
    rope_spec = pl.BlockSpec((PROMPT_TILE, HEAD_DIM), lambda b, t: (t, 0))
    return pl.pallas_call(
        _mix_prompt_kernel,
        grid=(bsz, nt),
        in_specs=[
            pl.BlockSpec((1, PROMPT_TILE, D_MODEL), lambda b, t: (b, t, 0)),
            _const_spec((D_MODEL, IN_COLS)),
            _const_spec((2, WIDTH)),
            _const_spec((1, WIDTH)),
            _const_spec((1, WIDTH)),
            rope_spec,
            rope_spec,
            _const_spec((CHUNK, 3 * CHUNK)),
            _const_spec((len(LEVELS) + 2, CHUNK, CHUNK)),
            _const_spec((HEADS, CHUNK, CHUNK)),
            _const_spec((CHUNK, WIDTH)),
            _const_spec((CHUNK, WIDTH)),
            _const_spec((1, WIDTH)),
        ],
        out_specs=[
            pl.BlockSpec((1, PROMPT_TILE, 2 * WIDTH), lambda b, t: (b, t, 0)),
            state_spec,
            state_spec,
        ],
        out_shape=[jax.ShapeDtypeStruct((bsz, seq, 2 * WIDTH), BF16), state_shape, state_shape],
        scratch_shapes=[act] * 10 + [pltpu.VMEM((HEADS, HEAD_DIM, HEAD_DIM), F32)],
        compiler_params=pltpu.CompilerParams(
            dimension_semantics=("arbitrary", "arbitrary"), vmem_limit_bytes=VMEM_LIMIT),
        name="mix_prompt",
    )(x, w_in_b, lb_logits, gnh, gnr, cos2, sin2, tril3, masks, dmat, cross_l, upd_l, cdec_l)


def _inproj_sample_kernel(x_ref, w_ref, lbl_ref, cos_ref, sin_ref,
                          qh, kh, vh, gh, gth, qr, kr, vr, gtr):
    lb = _lower_bound(lbl_ref)
    _project_activate(x_ref[...].astype(BF16), w_ref, lb, cos_ref[...], sin_ref[...],
                      qh, kh, vh, gh, gth, qr, kr, vr, gtr)


def _inproj_sample(x2d, w_in_b, lb_logits, cos2, sin2):
    n = x2d.shape[0]
    return pl.pallas_call(
        _inproj_sample_kernel,
        out_shape=(jax.ShapeDtypeStruct((n, WIDTH), F32),) * 9,
        compiler_params=pltpu.CompilerParams(vmem_limit_bytes=VMEM_LIMIT),
        name="inproj_sample",
    )(x2d, w_in_b, lb_logits, cos2, sin2)


def _mix_sample_kernel(rc_ref, qh, kh, vh, gh, gth, qr, kr, vr, gtr, shin, srin, gnh_ref, gnr_ref,
                       mix_ref, shout, srout, qd_s, kd_s, v_s, o_s):
    nseq = SAMPLE_SEQS
    nt = qh.shape[0]
    pad_rows = jnp.zeros((HEAD_DIM - nseq, HEAD_DIM), F32)

    def per_sequence(h, s_in, s_out, decay):
        for bb in range(nseq):
            seq_rows = pl.ds(bb, nt, stride=nseq)
            s = s_in[bb, h]
            o_s[seq_rows, :] = _dot(qd_s[seq_rows, :].astype(BF16), s.astype(BF16))
            s_out[bb, h] = decay(s, bb) + _dot_tn(kd_s[seq_rows, :].astype(BF16),
                                                  v_s[seq_rows, :].astype(BF16))

    for h in range(HEADS):
        hs = slice(h * HEAD_DIM, (h + 1) * HEAD_DIM)
        q = [qh[t, :, hs] for t in range(nt)]
        k = [kh[t, :, hs] for t in range(nt)]
        v = [vh[t, :, hs] for t in range(nt)]
        b = []
        for t in range(nt):
            g = gh[t, :, hs]
            b.append(g if t == 0 else b[-1] + g)
        b_last = b[-1]
        o = []
        for i in range(nt):
            acc = None
            for j in range(i + 1):
                w = q[i] * k[j]
                if j < i:
                    w = w * jnp.exp2(b[i] - b[j])
                term = jnp.sum(w, axis=-1, keepdims=True) * v[j]
                acc = term if acc is None else acc + term
            o.append(acc)
        for t in range(nt):
            tr = slice(t * nseq, (t + 1) * nseq)
            qd_s[tr, :] = q[t] * jnp.exp2(b[t])
            kd_s[tr, :] = k[t] * jnp.exp2(b_last - b[t])
            v_s[tr, :] = v[t]
        dcol = jnp.concatenate([jnp.exp2(b_last), pad_rows], axis=0).T
        per_sequence(h, shin, shout, lambda s, bb: s * dcol[:, bb:bb + 1])
        for t in range(nt):
            tr = slice(t * nseq, (t + 1) * nseq)
            mix_ref[t, :, hs] = _head_norm_gate(o[t] + o_s[tr, :], gnh_ref[:, hs],
                                                gth[t, :, hs]).astype(BF16)
        q = [qr[t, :, hs] for t in range(nt)]
        k = [kr[t, :, hs] for t in range(nt)]
        v = [vr[t, :, hs] for t in range(nt)]
        o = []
        for i in range(nt):
            acc = None
            for j in range(i + 1):
                a_ij = jnp.sum(q[i] * k[j], axis=-1, keepdims=True) * rc_ref[h, i * nt + j]
                term = a_ij * v[j]
                acc = term if acc is None else acc + term
            o.append(acc)
        for t in range(nt):
            tr = slice(t * nseq, (t + 1) * nseq)
            qd_s[tr, :] = q[t]
            kd_s[tr, :] = k[t] * rc_ref[h, nt * nt + nt + t]
            v_s[tr, :] = v[t]
        cdec = rc_ref[h, nt * nt + 2 * nt]
        per_sequence(h, srin, srout, lambda s, bb: s * cdec)
        rs = slice(WIDTH + h * HEAD_DIM, WIDTH + (h + 1) * HEAD_DIM)
        for t in range(nt):
            tr = slice(t * nseq, (t + 1) * nseq)
            cross_t = rc_ref[h, nt * nt + t]
            mix_ref[t, :, rs] = _head_norm_gate(o[t] + o_s[tr, :] * cross_t, gnr_ref[:, hs],
                                                gtr[t, :, hs]).astype(BF16)


def _mix_sample(acts, state_h, state_r, gnh, gnr, nt, nb):
    dmat, cross, upd, cdec = _ret_consts(nt)
    rc = jnp.concatenate([dmat.reshape(HEADS, nt * nt), cross, upd, cdec[:, None]], axis=1)
    acts3 = [a.reshape(nt, nb, WIDTH) for a in acts]
    act_spec = pl.BlockSpec((nt, SAMPLE_SEQS, WIDTH), lambda i: (0, i, 0))
    st_spec = pl.BlockSpec((SAMPLE_SEQS, HEADS, HEAD_DIM, HEAD_DIM), lambda i: (i, 0, 0, 0))
    st_shape = jax.ShapeDtypeStruct((nb, HEADS, HEAD_DIM, HEAD_DIM), F32)
    row_s = pltpu.VMEM((nt * SAMPLE_SEQS, HEAD_DIM), F32)
    return pl.pallas_call(
        _mix_sample_kernel,
        grid=(nb // SAMPLE_SEQS,),
        in_specs=[pl.BlockSpec(memory_space=pltpu.SMEM)] + [act_spec] * 9 + [st_spec, st_spec]
                 + [pl.BlockSpec((1, WIDTH), lambda i: (0, 0))] * 2,
        out_specs=[pl.BlockSpec((nt, SAMPLE_SEQS, 2 * WIDTH), lambda i: (0, i, 0)), st_spec, st_spec],
        out_shape=[jax.ShapeDtypeStruct((nt, nb, 2 * WIDTH), BF16), st_shape, st_shape],
        scratch_shapes=[row_s] * 4,
        compiler_params=pltpu.CompilerParams(
            dimension_semantics=("arbitrary",), vmem_limit_bytes=VMEM_LIMIT),
        name="mix_sample",
    )(rc, *acts3, state_h, state_r, gnh, gnr)


def _tail_kernel(mix_ref, x_ref, wo_ref, g1_ref, b1_ref, wg_ref, wu_ref, wd_ref, g2_ref, b2_ref,
                 y_ref):
    m = _dot(mix_ref[...], wo_ref[...])
    x1 = _layer_norm(DEEPNORM_ALPHA * x_ref[...] + m, g1_ref[...], b1_ref[...])
    xb = x1.astype(BF16)
    hidden = []
    for j in range(0, D_FF, FF_TILE):
        cols = slice(j, j + FF_TILE)
        hidden.append((_silu(_dot(xb, wg_ref[:, cols])) * _dot(xb, wu_ref[:, cols])).astype(BF16))
    d = _dot(jnp.concatenate(hidden, axis=1), wd_ref[...])
    y_ref[...] = _layer_norm(DEEPNORM_ALPHA * x1 + d, g2_ref[...], b2_ref[...])


def _tail(tile, mix2d, x2d, wo, g1, b1, wg, wu, wd, g2, b2):
    n = x2d.shape[0]
    row_spec = lambda: pl.BlockSpec((tile, D_MODEL), lambda i: (i, 0))
    return pl.pallas_call(
        _tail_kernel,
        grid=(n // tile,),
        in_specs=[row_spec(), row_spec(),
                  _const_spec((D_MODEL, D_MODEL)), _const_spec((1, D_MODEL)), _const_spec((1, D_MODEL)),
                  _const_spec((D_MODEL, D_FF)), _const_spec((D_MODEL, D_FF)), _const_spec((D_FF, D_MODEL)),
                  _const_spec((1, D_MODEL)), _const_spec((1, D_MODEL))],
        out_specs=row_spec(),
        out_shape=jax.ShapeDtypeStruct((n, D_MODEL), F32),
        compiler_params=pltpu.CompilerParams(
            dimension_semantics=("arbitrary",), vmem_limit_bytes=VMEM_LIMIT),
        name="tail",
    )(mix2d, x2d, wo, g1, b1, wg, wu, wd, g2, b2)


def kernel(x_prompt, x_sample, state_hgrn, state_ret, w_in, lb_logits, hgrn_norm_g, ret_norm_g,
           w_out, ln1_g, ln1_b, w_gate, w_up, w_down, ln2_g, ln2_b):
    bsz, seq, _ = x_prompt.shape
    nb, nt, _ = x_sample.shape
    w_in_b = w_in[0].astype(BF16)
    tail_w = (w_out[0].astype(BF16), ln1_g, ln1_b, w_gate[0].astype(BF16), w_up[0].astype(BF16),
              w_down[0].astype(BF16), ln2_g, ln2_b)
    gnh = hgrn_norm_g[0].reshape(1, WIDTH)
    gnr = ret_norm_g[0].reshape(1, WIDTH)

    mix_p, sh_p, sr_p = _mix_prompt(x_prompt, w_in_b, lb_logits, gnh, gnr)
    y_p = _tail(TAIL_TILE, mix_p.reshape(bsz * seq, 2 * WIDTH), x_prompt.reshape(bsz * seq, D_MODEL),
                *tail_w)

    xs = jnp.transpose(x_sample, (1, 0, 2)).reshape(nt * nb, D_MODEL)
    cos2, sin2 = _rope_tables(jnp.arange(nt, dtype=F32) + float(PAST_LEN), jnp.zeros((nb,), F32))
    acts = _inproj_sample(xs, w_in_b, lb_logits, cos2, sin2)
    mix_s, sh_s, sr_s = _mix_sample(acts, state_hgrn[0], state_ret[0], gnh, gnr, nt, nb)
    y_s = _tail(nt * nb, mix_s.reshape(nt * nb, 2 * WIDTH), xs, *tail_w)
    y_s = jnp.transpose(y_s.reshape(nt, nb, D_MODEL), (1, 0, 2))

    return (y_p.reshape(bsz, seq, D_MODEL), y_s, sh_p[None], sr_p[None], sh_s[None], sr_s[None])
```

```python
import numpy as np
import jax
import jax.numpy as jnp
from jax import lax
from jax.experimental import pallas as pl
from jax.experimental.pallas import tpu as pltpu

F32 = jnp.float32
BF16 = jnp.bfloat16

D_MODEL = 1024
HEADS = 4
HEAD_DIM = 128
WIDTH = HEADS * HEAD_DIM
IN_COLS = 8 * WIDTH
D_FF = 2816
CHUNK = 128
ROPE_BASE = 10000.0
PAST_LEN = 16384
DEEPNORM_ALPHA = 2.0 ** 0.25
EPS = 1e-5
LOG2_E = 1.4426950408889634

LANES = 128
SUBLANES = 8
PROMPT_TILE = 1024
TAIL_TILE = 1024
TAIL_ROWS = 512
FF_TILE = 256
SAMPLE_SEQS = 16
VMEM_LIMIT = 60 * 1024 * 1024

LEVELS = (64, 32, 16, 8, 4, 2, 1)
FAST_BLOCK = 64
FAST_SPAN_BITS = 100.0
FAST_Q_MAX = 2.0 ** 20


def _sigmoid_pair(x):
    e = jnp.exp(-jnp.abs(x))
    r = 1.0 / (1.0 + e)
    er = e * r
    pos = x >= 0
    return jnp.where(pos, r, er), jnp.where(pos, er, r)


def _silu(x):
    return x / (1.0 + jnp.exp2(x * -LOG2_E))


def _dot(a, b):
    return jnp.dot(a, b, preferred_element_type=F32)


def _dot_nt(a, b):
    return lax.dot_general(a, b, (((1,), (1,)), ((), ())), preferred_element_type=F32)


def _dot_tn(a, b):
    return lax.dot_general(a, b, (((0,), (0,)), ((), ())), preferred_element_type=F32)


def _split3(x):
    hi = x.astype(BF16)
    r = x - hi.astype(F32)
    mid = r.astype(BF16)
    lo = (r - mid.astype(F32)).astype(BF16)
    return hi, mid, lo


def _lower_bound(lbl_ref):
    l = lbl_ref[...]
    l0 = l[0:1, :]
    l1 = l[1:2, :]
    m = jnp.maximum(l0, l1)
    e0 = jnp.exp(l0 - m)
    e1 = jnp.exp(l1 - m)
    return e0 / (e0 + e1)


def _head_norm_gate(o, gain, gate):
    ms = jnp.mean(o * o, axis=-1, keepdims=True)
    return o * lax.rsqrt(ms + EPS) * gain * gate


def _layer_norm(h, g, b):
    mu = jnp.mean(h, axis=-1, keepdims=True)
    d = h - mu
    var = jnp.mean(d * d, axis=-1, keepdims=True)
    return d * lax.rsqrt(var + EPS) * g + b


def _project_activate(xb, w_ref, lb, cos2, sin2, qh, kh, vh, gh, gth, qr, kr, vr, gtr):
    def proj(g):
        return _dot(xb, w_ref[:, g * WIDTH:(g + 1) * WIDTH])

    qh[...] = _silu(proj(0))
    s_pos, s_neg = _sigmoid_pair(proj(1))
    gh[...] = jnp.log(lb + (1.0 - lb) * s_pos) * LOG2_E
    kh[...] = (1.0 - lb) * s_neg
    vh[...] = proj(2)
    gth[...] = _silu(proj(3))

    def rotary(p, scale):
        for h in range(HEADS):
            hs = slice(h * HEAD_DIM, (h + 1) * HEAD_DIM)
            xh = p[:, hs]
            r = xh * cos2 + pltpu.roll(xh, HEAD_DIM // 2, 1) * sin2
            yield hs, (r if scale is None else r * scale)

    for hs, r in rotary(proj(4), None):
        qr[:, hs] = r
    for hs, r in rotary(proj(5), HEAD_DIM ** -0.5):
        kr[:, hs] = r
    vr[...] = proj(6)
    gtr[...] = _silu(proj(7))


def _rope_kernel(hi_ref, lo_ref, inv_ref, sign_ref, cos_ref, sin_ref):
    inv = inv_ref[...]
    a = hi_ref[...] * inv
    b = lo_ref[...] * inv
    ca, sa = jnp.cos(a), jnp.sin(a)
    cb, sb = jnp.cos(b), jnp.sin(b)
    n_lo = lo_ref.shape[0]
    for i in range(hi_ref.shape[0]):
        rows = slice(i * n_lo, (i + 1) * n_lo)
        cos_ref[rows, :] = ca[i:i + 1, :] * cb - sa[i:i + 1, :] * sb
        sin_ref[rows, :] = (sa[i:i + 1, :] * cb + ca[i:i + 1, :] * sb) * sign_ref[...]


def _rope_tables(pos_hi, pos_lo):
    n = pos_hi.shape[0] * pos_lo.shape[0]
    half = HEAD_DIM // 2
    inv = ROPE_BASE ** (-jnp.arange(half, dtype=F32) / half)
    inv2 = jnp.concatenate([inv, inv])[None, :]
    sign = jnp.asarray(np.concatenate([-np.ones(half), np.ones(half)])[None, :], F32)
    return pl.pallas_call(
        _rope_kernel,
        out_shape=(jax.ShapeDtypeStruct((n, HEAD_DIM), F32),) * 2,
        name="rope_tables",
    )(pos_hi[:, None], pos_lo[:, None], inv2, sign)


def _ret_consts(c):
    lg = jnp.log1p(-jnp.exp2(-5.0 - jnp.arange(HEADS, dtype=F32)))
    idx = jnp.arange(c, dtype=F32)
    mask = jnp.tril(jnp.ones((c, c), dtype=bool))
    rel = jnp.where(mask, idx[:, None] - idx[None, :], 0.0)
    dmat = jnp.where(mask[None], jnp.exp(lg[:, None, None] * rel[None]), 0.0)
    cross = jnp.exp(lg[:, None] * (idx + 1.0)[None, :])
    upd = jnp.exp(lg[:, None] * (c - 1.0 - idx)[None, :])
    cdec = jnp.exp(lg * c)
    return dmat, cross, upd, cdec


def _level_masks():
    i = np.arange(CHUNK)[:, None]
    j = np.arange(CHUNK)[None, :]
    ms = [(i == j)]
    for s in LEVELS:
        ms.append((i // (2 * s) == j // (2 * s)) & (i % (2 * s) >= s) & (j % (2 * s) < s))
    ms.append((i // FAST_BLOCK == j // FAST_BLOCK) & (i >= j))
    return np.stack(ms).astype(np.float32)


def _level_factor(s, b, q, k, sub):
    def row(r):
        return jnp.broadcast_to(b[r:r + 1, :], (SUBLANES, LANES))

    es, ws = [], []
    for m in range(CHUNK // SUBLANES):
        r0 = m * SUBLANES
        qb = q[r0:r0 + SUBLANES, :]
        kb = k[r0:r0 + SUBLANES, :]
        bb = b[r0:r0 + SUBLANES, :]
        if s >= SUBLANES:
            ref = row((r0 // (2 * s)) * 2 * s + s - 1)
            lower = (r0 % (2 * s)) >= s
            es.append(bb - ref if lower else ref - bb)
            ws.append(qb if lower else kb)
        elif s == 4:
            es.append(-jnp.abs(bb - row(r0 + 3)))
            ws.append(jnp.where(sub >= 4, qb, kb))
        elif s == 2:
            es.append(-jnp.abs(bb - jnp.where(sub < 4, row(r0 + 1), row(r0 + 5))))
            ws.append(jnp.where((sub & 3) >= 2, qb, kb))
        else:
            ref = jnp.where(sub < 2, row(r0),
                            jnp.where(sub < 4, row(r0 + 2), jnp.where(sub < 6, row(r0 + 4), row(r0 + 6))))
            es.append(-jnp.abs(bb - ref))
            ws.append(jnp.where((sub & 1) == 1, qb, kb))
    e = jnp.concatenate(es, axis=0)
    w = jnp.concatenate(ws, axis=0)
    return (w * jnp.exp2(e)).astype(BF16)


def _fast_block_factors(b, q, k):
    refs = []
    for blk in range(CHUNK // FAST_BLOCK):
        r = blk * FAST_BLOCK + FAST_BLOCK // 2 - 1
        refs.append(jnp.broadcast_to(b[r:r + 1, :], (FAST_BLOCK, LANES)))
    e = b - jnp.concatenate(refs, axis=0)
    return (q * jnp.exp2(e)).astype(BF16), (k * jnp.exp2(-e)).astype(BF16)


def _mix_chunks(fast, acts, b_s, sht, sr_ref, mix_ref, gnh_ref, gnr_ref, masks_ref, dmat_ref,
                cross_ref, upd_ref, cdec_ref):
    qh, kh, vh, _, gth, qr, kr, vr, gtr = acts
    sub = lax.broadcasted_iota(jnp.int32, (SUBLANES, LANES), 0)
    for ci in range(PROMPT_TILE // CHUNK):
        rows = slice(ci * CHUNK, (ci + 1) * CHUNK)
        for h in range(HEADS):
            hs = slice(h * HEAD_DIM, (h + 1) * HEAD_DIM)
            q = qh[rows, hs]
            k = kh[rows, hs]
            v = vh[rows, hs].astype(BF16)
            b = b_s[rows, hs]
            b_last = b[CHUNK - 1:CHUNK, :]
            if fast:
                qf, kf = _fast_block_factors(b, q, k)
                a = jnp.where(masks_ref[len(LEVELS) + 1] > 0.5, _dot_nt(qf, kf), 0.0)
            else:
                a = _dot_nt(q.astype(BF16), k.astype(BF16)) * masks_ref[0]
            for li, s in enumerate(LEVELS):
                if fast and s < FAST_BLOCK:
                    continue
                xl = _level_factor(s, b, q, k, sub)
                a = a + _dot_nt(xl, xl) * masks_ref[li + 1]
            st = sht[h]
            o = _dot(a.astype(BF16), v) + _dot_nt((q * jnp.exp2(b)).astype(BF16), st.astype(BF16))
            kd = (k * jnp.exp2(b_last - b)).astype(BF16)
            sht[h] = st * jnp.exp2(b_last) + _dot_tn(v, kd)
            mix_ref[0, rows, hs] = _head_norm_gate(o, gnh_ref[:, hs], gth[rows, hs]).astype(BF16)
            q = qr[rows, hs]
            k = kr[rows, hs]
            v = vr[rows, hs].astype(BF16)
            a = _dot_nt(q.astype(BF16), k.astype(BF16)) * dmat_ref[h]
            s_r = sr_ref[0, h]
            qa = jnp.concatenate([(q * cross_ref[:, hs]).astype(BF16), a.astype(BF16)], axis=1)
            o = _dot(qa, jnp.concatenate([s_r.astype(BF16), v], axis=0))
            ku = (k * upd_ref[:, hs]).astype(BF16)
            sr_ref[0, h] = s_r * cdec_ref[:, hs] + _dot_tn(ku, v)
            rs = slice(WIDTH + h * HEAD_DIM, WIDTH + (h + 1) * HEAD_DIM)
            mix_ref[0, rows, rs] = _head_norm_gate(o, gnr_ref[:, hs], gtr[rows, hs]).astype(BF16)


def _mix_prompt_kernel(x_ref, w_ref, lbl_ref, gnh_ref, gnr_ref, cos_ref, sin_ref,
                       tril_ref, masks_ref, dmat_ref, cross_ref, upd_ref, cdec_ref,
                       mix_ref, sh_ref, sr_ref, *scratch):
    acts, b_s, sht = scratch[:9], scratch[9], scratch[10]
    t = pl.program_id(1)

    @pl.when(t == 0)
    def _():
        sht[...] = jnp.zeros_like(sht)
        sr_ref[...] = jnp.zeros_like(sr_ref)

    lb = _lower_bound(lbl_ref)
    _project_activate(x_ref[0].astype(BF16), w_ref, lb, cos_ref[...], sin_ref[...], *acts)

    qh, gh = acts[0], acts[3]
    span = None
    for ci in range(PROMPT_TILE // CHUNK):
        rows = slice(ci * CHUNK, (ci + 1) * CHUNK)
        b_all = _dot(tril_ref[...], jnp.concatenate(_split3(gh[rows, :]), axis=0))
        b_s[rows, :] = b_all
        prev = None
        for r in range(FAST_BLOCK // 2 - 1, CHUNK, FAST_BLOCK // 2):
            cur = b_all[r:r + 1, :]
            d = -cur if prev is None else prev - cur
            span = d if span is None else jnp.maximum(span, d)
            prev = cur
    fast = (jnp.max(span) < FAST_SPAN_BITS) & (jnp.max(jnp.abs(qh[...])) < FAST_Q_MAX)
    mix_args = (acts, b_s, sht, sr_ref, mix_ref, gnh_ref, gnr_ref, masks_ref, dmat_ref, cross_ref,
                upd_ref, cdec_ref)

    @pl.when(fast)
    def _():
        _mix_chunks(True, *mix_args)

    @pl.when(jnp.logical_not(fast))
    def _():
        _mix_chunks(False, *mix_args)

    @pl.when(t == pl.num_programs(1) - 1)
    def _():
        for h in range(HEADS):
            sh_ref[0, h] = sht[h].T


def _const_spec(shape):
    nd = len(shape)
    return pl.BlockSpec(shape, lambda *_: (0,) * nd, pipeline_mode=pl.Buffered(1))


def _mix_prompt(x, w_in_b, lb_logits, gnh, gnr):
    bsz, seq, _ = x.shape
    nt = seq // PROMPT_TILE
    cos2, sin2 = _rope_tables(jnp.arange(0, seq, LANES, dtype=F32), jnp.arange(LANES, dtype=F32))
    dmat, cross, upd, cdec = _ret_consts(CHUNK)
    lane = lambda a: jnp.repeat(a, HEAD_DIM, axis=-1)
    cross_l = lane(cross.T)
    upd_l = lane(upd.T)
    cdec_l = lane(cdec[None, :])
    tril = np.tril(np.ones((CHUNK, CHUNK), np.float32))
    tril3 = jnp.asarray(np.concatenate([tril] * 3, axis=1), BF16)
    masks = jnp.asarray(_level_masks())
    act = pltpu.VMEM((PROMPT_TILE, WIDTH), F32)
    state_shape = jax.ShapeDtypeStruct((bsz, HEADS, HEAD_DIM, HEAD_DIM), F32)
    state_spec = pl.BlockSpec((1, HEADS, HEAD_DIM, HEAD_DIM), lambda b, t: (b, 0, 0, 0))
    rope_spec = pl.BlockSpec((PROMPT_TILE, HEAD_DIM), lambda b, t: (t, 0))
    return pl.pallas_call(
        _mix_prompt_kernel,
        grid=(bsz, nt),
        in_specs=[
            pl.BlockSpec((1, PROMPT_TILE, D_MODEL), lambda b, t: (b, t, 0)),
            _const_spec((D_MODEL, IN_COLS)),
            _const_spec((2, WIDTH)),
            _const_spec((1, WIDTH)),
            _const_spec((1, WIDTH)),
            rope_spec,
            rope_spec,
            _const_spec((CHUNK, 3 * CHUNK)),
            _const_spec((len(LEVELS) + 2, CHUNK, CHUNK)),
            _const_spec((HEADS, CHUNK, CHUNK)),
            _const_spec((CHUNK, WIDTH)),
            _const_spec((CHUNK, WIDTH)),
            _const_spec((1, WIDTH)),
        ],
        out_specs=[
            pl.BlockSpec((1, PROMPT_TILE, 2 * WIDTH), lambda b, t: (b, t, 0)),
            state_spec,
            state_spec,
        ],
        out_shape=[jax.ShapeDtypeStruct((bsz, seq, 2 * WIDTH), BF16), state_shape, state_shape],
        scratch_shapes=[act] * 10 + [pltpu.VMEM((HEADS, HEAD_DIM, HEAD_DIM), F32)],
        compiler_params=pltpu.CompilerParams(
            dimension_semantics=("arbitrary", "arbitrary"), vmem_limit_bytes=VMEM_LIMIT),
        name="mix_prompt",
    )(x, w_in_b, lb_logits, gnh, gnr, cos2, sin2, tril3, masks, dmat, cross_l, upd_l, cdec_l)


def _inproj_sample_kernel(x_ref, w_ref, lbl_ref, cos_ref, sin_ref,
                          qh, kh, vh, gh, gth, qr, kr, vr, gtr):
    lb = _lower_bound(lbl_ref)
    _project_activate(x_ref[...].astype(BF16), w_ref, lb, cos_ref[...], sin_ref[...],
                      qh, kh, vh, gh, gth, qr, kr, vr, gtr)


def _inproj_sample(x2d, w_in_b, lb_logits, cos2, sin2):
    n = x2d.shape[0]
    return pl.pallas_call(
        _inproj_sample_kernel,
        out_shape=(jax.ShapeDtypeStruct((n, WIDTH), F32),) * 9,
        compiler_params=pltpu.CompilerParams(vmem_limit_bytes=VMEM_LIMIT),
        name="inproj_sample",
    )(x2d, w_in_b, lb_logits, cos2, sin2)


def _mix_sample_kernel(rc_ref, qh, kh, vh, gh, gth, qr, kr, vr, gtr, shin, srin, gnh_ref, gnr_ref,
                       mix_ref, shout, srout, qd_s, kd_s, v_s, o_s):
    nseq = SAMPLE_SEQS
    nt = qh.shape[0]
    pad_rows = jnp.zeros((HEAD_DIM - nseq, HEAD_DIM), F32)

    def per_sequence(h, s_in, s_out, decay):
        for bb in range(nseq):
            seq_rows = pl.ds(bb, nt, stride=nseq)
            s = s_in[bb, h]
            o_s[seq_rows, :] = _dot(qd_s[seq_rows, :].astype(BF16), s.astype(BF16))
            s_out[bb, h] = decay(s, bb) + _dot_tn(kd_s[seq_rows, :].astype(BF16),
                                                  v_s[seq_rows, :].astype(BF16))

    for h in range(HEADS):
        hs = slice(h * HEAD_DIM, (h + 1) * HEAD_DIM)
        q = [qh[t, :, hs] for t in range(nt)]
        k = [kh[t, :, hs] for t in range(nt)]
        v = [vh[t, :, hs] for t in range(nt)]
        b = []
        for t in range(nt):
            g = gh[t, :, hs]
            b.append(g if t == 0 else b[-1] + g)
        b_last = b[-1]
        o = []
        for i in range(nt):
            acc = None
            for j in range(i + 1):
                w = q[i] * k[j]
                if j < i:
                    w = w * jnp.exp2(b[i] - b[j])
                term = jnp.sum(w, axis=-1, keepdims=True) * v[j]
                acc = term if acc is None else acc + term
            o.append(acc)
        for t in range(nt):
            tr = slice(t * nseq, (t + 1) * nseq)
            qd_s[tr, :] = q[t] * jnp.exp2(b[t])
            kd_s[tr, :] = k[t] * jnp.exp2(b_last - b[t])
            v_s[tr, :] = v[t]
        dcol = jnp.concatenate([jnp.exp2(b_last), pad_rows], axis=0).T
        per_sequence(h, shin, shout, lambda s, bb: s * dcol[:, bb:bb + 1])
        for t in range(nt):
            tr = slice(t * nseq, (t + 1) * nseq)
            mix_ref[t, :, hs] = _head_norm_gate(o[t] + o_s[tr, :], gnh_ref[:, hs],
                                                gth[t, :, hs]).astype(BF16)
        q = [qr[t, :, hs] for t in range(nt)]
        k = [kr[t, :, hs] for t in range(nt)]
        v = [vr[t, :, hs] for t in range(nt)]
        o = []
        for i in range(nt):
            acc = None
            for j in range(i + 1):
                a_ij = jnp.sum(q[i] * k[j], axis=-1, keepdims=True) * rc_ref[h, i * nt + j]
                term = a_ij * v[j]
                acc = term if acc is None else acc + term
            o.append(acc)
        for t in range(nt):
            tr = slice(t * nseq, (t + 1) * nseq)
            qd_s[tr, :] = q[t]
            kd_s[tr, :] = k[t] * rc_ref[h, nt * nt + nt + t]
            v_s[tr, :] = v[t]
        cdec = rc_ref[h, nt * nt + 2 * nt]
        per_sequence(h, srin, srout, lambda s, bb: s * cdec)
        rs = slice(WIDTH + h * HEAD_DIM, WIDTH + (h + 1) * HEAD_DIM)
        for t in range(nt):
            tr = slice(t * nseq, (t + 1) * nseq)
            cross_t = rc_ref[h, nt * nt + t]
            mix_ref[t, :, rs] = _head_norm_gate(o[t] + o_s[tr, :] * cross_t, gnr_ref[:, hs],
                                                gtr[t, :, hs]).astype(BF16)


def _mix_sample(acts, state_h, state_r, gnh, gnr, nt, nb):
    dmat, cross, upd, cdec = _ret_consts(nt)
    rc = jnp.concatenate([dmat.reshape(HEADS, nt * nt), cross, upd, cdec[:, None]], axis=1)
    acts3 = [a.reshape(nt, nb, WIDTH) for a in acts]
    act_spec = pl.BlockSpec((nt, SAMPLE_SEQS, WIDTH), lambda i: (0, i, 0))
    st_spec = pl.BlockSpec((SAMPLE_SEQS, HEADS, HEAD_DIM, HEAD_DIM), lambda i: (i, 0, 0, 0))
    st_shape = jax.ShapeDtypeStruct((nb, HEADS, HEAD_DIM, HEAD_DIM), F32)
    row_s = pltpu.VMEM((nt * SAMPLE_SEQS, HEAD_DIM), F32)
    return pl.pallas_call(
        _mix_sample_kernel,
        grid=(nb // SAMPLE_SEQS,),
        in_specs=[pl.BlockSpec(memory_space=pltpu.SMEM)] + [act_spec] * 9 + [st_spec, st_spec]
                 + [pl.BlockSpec((1, WIDTH), lambda i: (0, 0))] * 2,
        out_specs=[pl.BlockSpec((nt, SAMPLE_SEQS, 2 * WIDTH), lambda i: (0, i, 0)), st_spec, st_spec],
        out_shape=[jax.ShapeDtypeStruct((nt, nb, 2 * WIDTH), BF16), st_shape, st_shape],
        scratch_shapes=[row_s] * 4,
        compiler_params=pltpu.CompilerParams(
            dimension_semantics=("arbitrary",), vmem_limit_bytes=VMEM_LIMIT),
        name="mix_sample",
    )(rc, *acts3, state_h, state_r, gnh, gnr)


def _tail_kernel(mix_ref, x_ref, wo_ref, g1_ref, b1_ref, wg_ref, wu_ref, wd_ref, g2_ref, b2_ref,
                 y_ref):
    groups = [slice(r, r + TAIL_ROWS) for r in range(0, mix_ref.shape[0], TAIL_ROWS)]

    def out_proj(rows):
        return DEEPNORM_ALPHA * x_ref[rows, :] + _dot(mix_ref[rows, :], wo_ref[...])

    def gate_up(x1):
        xb = x1.astype(BF16)
        hidden = []
        for j in range(0, D_FF, FF_TILE):
            cols = slice(j, j + FF_TILE)
            hidden.append((_silu(_dot(xb, wg_ref[:, cols])) * _dot(xb, wu_ref[:, cols])).astype(BF16))
        return jnp.concatenate(hidden, axis=1)

    x1s = [_layer_norm(out_proj(rows), g1_ref[...], b1_ref[...]) for rows in groups]
    hids = [gate_up(x1) for x1 in x1s]
    for rows, x1, hid in zip(groups, x1s, hids):
        d = _dot(hid, wd_ref[...])
        y_ref[rows, :] = _layer_norm(DEEPNORM_ALPHA * x1 + d, g2_ref[...], b2_ref[...])


def _tail(tile, mix2d, x2d, wo, g1, b1, wg, wu, wd, g2, b2):
    n = x2d.shape[0]
    row_spec = lambda: pl.BlockSpec((tile, D_MODEL), lambda i: (i, 0))
    return pl.pallas_call(
        _tail_kernel,
        grid=(n // tile,),
        in_specs=[row_spec(), row_spec(),
                  _const_spec((D_MODEL, D_MODEL)), _const_spec((1, D_MODEL)), _const_spec((1, D_MODEL)),
                  _const_spec((D_MODEL, D_FF)), _const_spec((D_MODEL, D_FF)), _const_spec((D_FF, D_MODEL)),
                  _const_spec((1, D_MODEL)), _const_spec((1, D_MODEL))],
        out_specs=row_spec(),
        out_shape=jax.ShapeDtypeStruct((n, D_MODEL), F32),
        compiler_params=pltpu.CompilerParams(
            dimension_semantics=("arbitrary",), vmem_limit_bytes=VMEM_LIMIT),
        name="tail",
    )(mix2d, x2d, wo, g1, b1, wg, wu, wd, g2, b2)


def kernel(x_prompt, x_sample, state_hgrn, state_ret, w_in, lb_logits, hgrn_norm_g, ret_norm_g,
           w_out, ln1_g, ln1_b, w_gate, w_up, w_down, ln2_g, ln2_b):
    bsz, seq, _ = x_prompt.shape
    nb, nt, _ = x_sample.shape
    w_in_b = w_in[0].astype(BF16)
    tail_w = (w_out[0].astype(BF16), ln1_g, ln1_b, w_gate[0].astype(BF16), w_up[0].astype(BF16),
              w_down[0].astype(BF16), ln2_g, ln2_b)
    gnh = hgrn_norm_g[0].reshape(1, WIDTH)
    gnr = ret_norm_g[0].reshape(1, WIDTH)

    mix_p, sh_p, sr_p = _mix_prompt(x_prompt, w_in_b, lb_logits, gnh, gnr)
    y_p = _tail(TAIL_TILE, mix_p.reshape(bsz * seq, 2 * WIDTH), x_prompt.reshape(bsz * seq, D_MODEL),
                *tail_w)

    xs = jnp.transpose(x_sample, (1, 0, 2)).reshape(nt * nb, D_MODEL)
    cos2, sin2 = _rope_tables(jnp.arange(nt, dtype=F32) + float(PAST_LEN), jnp.zeros((nb,), F32))
    acts = _inproj_sample(xs, w_in_b, lb_logits, cos2, sin2)
    mix_s, sh_s, sr_s = _mix_sample(acts, state_hgrn[0], state_ret[0], gnh, gnr, nt, nb)
    y_s = _tail(nt * nb, mix_s.reshape(nt * nb, 2 * WIDTH), xs, *tail_w)
    y_s = jnp.transpose(y_s.reshape(nt, nb, D_MODEL), (1, 0, 2))

    return (y_p.reshape(bsz, seq, D_MODEL), y_s, sh_p[None], sr_p[None], sh_s[None], sr_s[None])
```
